```python
import math
import jax, jax.numpy as jnp
from jax import lax
import numpy as np

D_MODEL = 2048
BATCH = 1
SEQ = 8192
DEPTH = 1

ATT_HEADS = 8
ATT_HEAD_DIM = 64
ATT_V_DIM = 2 * ATT_HEAD_DIM
ATT_QK_WIDTH = ATT_HEADS * 2 * ATT_HEAD_DIM
ATT_WIDTH = ATT_HEADS * ATT_V_DIM
CONV_WIDTH = 1024
CONV_K = 3
D_FF = 5632
FFN_K = 3
PLE_DIM = 256
IN_COLS = 3 * ATT_QK_WIDTH + 3 * CONV_WIDTH + 2 * D_MODEL
Q_BLOCK = 128
EPS = 1e-6

kernel_name = "hybrid_diffattn_shortconv_convffn_block"


def rmsnorm(x, g):
    xf = x.astype(jnp.float32)
    y = xf * lax.rsqrt(jnp.mean(xf * xf, axis=-1, keepdims=True) + EPS)
    return (y * g.astype(jnp.float32)).astype(x.dtype)


def causal_dwconv(u, w):
    k_width, channels = w.shape
    return lax.conv_general_dilated(
        u, w[:, None, :].astype(u.dtype), window_strides=(1,),
        padding=[(k_width - 1, 0)], dimension_numbers=("NWC", "WIO", "NWC"),
        feature_group_count=channels)


def alibi_slopes(n_heads):
    return jnp.asarray(2.0 ** (-8.0 * np.arange(1, n_heads + 1) / n_heads), dtype=jnp.float32)


def diff_attention(q, k, v, g_q, g_k, lq1, lk1, lq2, lk2, g_sub, lambda_init):
    b, s, h, _, d = q.shape
    q = rmsnorm(q, g_q) * (d ** -0.5)
    k = rmsnorm(k, g_k)
    lam = (jnp.exp(jnp.sum(lq1.astype(jnp.float32) * lk1.astype(jnp.float32)))
           - jnp.exp(jnp.sum(lq2.astype(jnp.float32) * lk2.astype(jnp.float32)))
           + lambda_init)
    slopes = alibi_slopes(h)
    n_blocks = s // Q_BLOCK
    qb = q.reshape(b, n_blocks, Q_BLOCK, h, 2, d).transpose(1, 0, 2, 3, 4, 5)
    k_pos = jnp.arange(s)

    def block(args):
        q_i, blk = args
        q_pos = blk * Q_BLOCK + jnp.arange(Q_BLOCK)
        scores = jnp.einsum("bqhcd,bkhcd->bhcqk", q_i, k,
                            preferred_element_type=jnp.float32)
        dist = (q_pos[:, None] - k_pos[None, :])
        bias = -slopes[:, None, None] * dist.astype(jnp.float32)[None]
        scores = scores + bias[None, :, None]
        scores = jnp.where((dist >= 0)[None, None, None], scores, -jnp.inf)
        probs = jax.nn.softmax(scores, axis=-1)
        attn = probs[:, :, 0] - lam * probs[:, :, 1]
        return jnp.einsum("bhqk,bkhe->bqhe", attn.astype(v.dtype), v)

    o = lax.map(block, (qb, jnp.arange(n_blocks)))
    o = o.transpose(1, 0, 2, 3, 4).reshape(b, s, h, 2 * d)
    o = rmsnorm(o, g_sub) * (1.0 - lambda_init)
    return o.reshape(b, s, h * 2 * d)


def setup_inputs(seed: int = 0) -> dict:
    key = jax.random.key(seed)
    ks = jax.random.split(key, 32)
    f32 = jnp.float32

    def nrm(k, shape, scale):
        return jax.random.normal(k, shape, f32) * scale

    def gain(k, shape):
        return 1.0 + 0.05 * jax.random.normal(k, shape, f32)

    L = DEPTH
    return {
        "x": jax.random.normal(ks[0], (BATCH, SEQ, D_MODEL), f32),
        "p": jax.random.normal(ks[1], (DEPTH, BATCH, SEQ, PLE_DIM), f32),
        "g_mix": gain(ks[2], (L, D_MODEL)),
        "w_in": nrm(ks[3], (L, D_MODEL, IN_COLS), D_MODEL ** -0.5),
        "b_gates": nrm(ks[4], (L, 2, D_MODEL), 0.01),
        "g_q": gain(ks[5], (L, ATT_HEAD_DIM)),
        "g_k": gain(ks[6], (L, ATT_HEAD_DIM)),
        "lam_q1": nrm(ks[7], (L, ATT_HEAD_DIM), 0.1),
        "lam_k1": nrm(ks[8], (L, ATT_HEAD_DIM), 0.1),
        "lam_q2": nrm(ks[9], (L, ATT_HEAD_DIM), 0.1),
        "lam_k2": nrm(ks[10], (L, ATT_HEAD_DIM), 0.1),
        "g_sub": gain(ks[11], (L, ATT_V_DIM)),
        "w_conv_mix": nrm(ks[12], (L, CONV_K, CONV_WIDTH), CONV_K ** -0.5),
        "w_att_out": nrm(ks[13], (L, ATT_WIDTH, D_MODEL), ATT_WIDTH ** -0.5),
        "w_conv_out": nrm(ks[14], (L, CONV_WIDTH, D_MODEL), CONV_WIDTH ** -0.5),
        "w_o": nrm(ks[15], (L, D_MODEL, D_MODEL), D_MODEL ** -0.5),
        "g_ffn": gain(ks[16], (L, D_MODEL)),
        "w_up": nrm(ks[17], (L, D_MODEL, 2 * D_FF), D_MODEL ** -0.5),
        "w_ffn_conv": nrm(ks[18], (L, FFN_K, 2 * D_FF), FFN_K ** -0.5),
        "w_down": nrm(ks[19], (L, D_FF, D_MODEL), D_FF ** -0.5),
        "w_ple": nrm(ks[20], (L, PLE_DIM, D_MODEL), PLE_DIM ** -0.5),
        "g_ple": gain(ks[21], (L, D_MODEL)),
        "g_pg": gain(ks[22], (L, D_MODEL)),
        "w_pg": nrm(ks[23], (L, D_MODEL, D_MODEL), D_MODEL ** -0.5),
        "b_pg": nrm(ks[24], (L, D_MODEL), 0.01),
    }


def reference(x, p, g_mix, w_in, b_gates, g_q, g_k, lam_q1, lam_k1, lam_q2, lam_k2,
              g_sub, w_conv_mix, w_att_out, w_conv_out, w_o, g_ffn, w_up, w_ffn_conv,
              w_down, w_ple, g_ple, g_pg, w_pg, b_pg):
    b, s, _ = x.shape
    splits = np.cumsum([ATT_QK_WIDTH, ATT_QK_WIDTH, ATT_WIDTH,
                        CONV_WIDTH, CONV_WIDTH, CONV_WIDTH, D_MODEL]).tolist()
    for i in range(DEPTH):
        lambda_init = 0.8 - 0.6 * math.exp(-0.3 * i)
        h = rmsnorm(x, g_mix[i])
        proj = h @ w_in[i]
        q, k, v, c_b, c_c, c_x, g_a, g_c = jnp.split(proj, splits, axis=-1)
        q = q.reshape(b, s, ATT_HEADS, 2, ATT_HEAD_DIM)
        k = k.reshape(b, s, ATT_HEADS, 2, ATT_HEAD_DIM)
        v = v.reshape(b, s, ATT_HEADS, ATT_V_DIM)
        y_att = diff_attention(q, k, v, g_q[i], g_k[i], lam_q1[i], lam_k1[i],
                               lam_q2[i], lam_k2[i], g_sub[i], lambda_init)
        y_conv = c_b * causal_dwconv(c_c * c_x, w_conv_mix[i])
        merged = (jax.nn.sigmoid(g_a + b_gates[i, 0]) * (y_att @ w_att_out[i])
                  + jax.nn.sigmoid(g_c + b_gates[i, 1]) * (y_conv @ w_conv_out[i]))
        x = x + merged @ w_o[i]
        u = causal_dwconv(rmsnorm(x, g_ffn[i]) @ w_up[i], w_ffn_conv[i])
        u_gate, u_val = jnp.split(u, 2, axis=-1)
        x = x + (jax.nn.silu(u_gate) * u_val) @ w_down[i]
        pe = rmsnorm(p[i] @ w_ple[i], g_ple[i])
        pg = jax.nn.sigmoid(rmsnorm(x, g_pg[i]) @ w_pg[i] + b_pg[i])
        x = x + pg * pe
    return x
```

```python
import functools
import math

import jax
import jax.numpy as jnp
from jax import lax
from jax.experimental import pallas as pl
from jax.experimental.pallas import tpu as pltpu

F32 = jnp.float32
BF16 = jnp.bfloat16

D_MODEL = 2048
ATT_HEADS = 8
ATT_HEAD_DIM = 64
ATT_V_DIM = 2 * ATT_HEAD_DIM
ATT_QK_WIDTH = ATT_HEADS * 2 * ATT_HEAD_DIM
ATT_WIDTH = ATT_HEADS * ATT_V_DIM
CONV_WIDTH = 1024
D_FF = 5632
PLE_DIM = 256
EPS = 1e-6

LANES = 128
CARRY_ROWS = 8
MASK_VALUE = -1e30
VMEM_LIMIT = 56 * 1024 * 1024

OFF_Q = 0
OFF_K = ATT_QK_WIDTH
OFF_V = 2 * ATT_QK_WIDTH
OFF_CB = OFF_V + ATT_WIDTH
OFF_CC = OFF_CB + CONV_WIDTH
OFF_CX = OFF_CC + CONV_WIDTH
OFF_GA = OFF_CX + CONV_WIDTH
OFF_GC = OFF_GA + D_MODEL

NT_DIMS = (((1,), (1,)), ((), ()))


def _rms_rows(xf, g):
    ms = jnp.mean(xf * xf, axis=-1, keepdims=True)
    return xf * lax.rsqrt(ms + EPS) * g


def _params(*sem):
    return pltpu.CompilerParams(dimension_semantics=sem, vmem_limit_bytes=VMEM_LIMIT)


def _resident(shape, index_map):
    return pl.BlockSpec(shape, index_map, pipeline_mode=pl.Buffered(1))


def _shift_rows(u, prev, rows):
    u1 = jnp.where(rows == 0, prev[CARRY_ROWS - 1:CARRY_ROWS], pltpu.roll(u, 1, 0))
    u2 = jnp.where(rows == 0, prev[CARRY_ROWS - 2:CARRY_ROWS - 1],
                   jnp.where(rows == 1, prev[CARRY_ROWS - 1:CARRY_ROWS], pltpu.roll(u, 2, 0)))
    return u1, u2


def _causal_conv3(u, w, prev):
    rows = lax.broadcasted_iota(jnp.int32, u.shape, 0)
    u1, u2 = _shift_rows(u, prev, rows)
    return w[0:1] * u2 + w[1:2] * u1 + w[2:3] * u


def _qk_kernel(x_ref, g_ref, w_ref, gain_ref, ones_ref, slope_ref, o_ref, h_ref, *, tm, tn, tk):
    i = pl.program_id(0)

    @pl.when(pl.program_id(1) == 0)
    def _():
        h_ref[...] = _rms_rows(x_ref[...], g_ref[...]).astype(BF16)

    acc = jnp.dot(h_ref[...], w_ref[...], preferred_element_type=F32)
    lane = lax.broadcasted_iota(jnp.int32, (tm, LANES), 1)
    row = lax.broadcasted_iota(jnp.int32, (tm, LANES), 0)
    col_in_tile = ((i * tm + row) & (tk - 1)).astype(F32)
    first = lane < ATT_HEAD_DIM
    for c in range(tn // LANES):
        sl = slice(c * LANES, (c + 1) * LANES)
        t = acc[:, sl]
        sq = t * t
        ss0 = jnp.sum(jnp.where(first, sq, 0.0), axis=-1, keepdims=True)
        ss1 = jnp.sum(jnp.where(first, 0.0, sq), axis=-1, keepdims=True)
        r = jnp.where(first, lax.rsqrt(ss0 / ATT_HEAD_DIM + EPS), lax.rsqrt(ss1 / ATT_HEAD_DIM + EPS))
        tnorm = t * r * gain_ref[:, sl]
        bias = slope_ref[:, sl] * col_in_tile
        hi = bias.astype(BF16).astype(F32)
        rem = bias - hi
        mid = rem.astype(BF16).astype(F32)
        lo = rem - mid
        aug = ones_ref[:, sl] + jnp.where(lane == ATT_HEAD_DIM, hi,
                                          jnp.where(lane == ATT_HEAD_DIM + 1, mid,
                                                    jnp.where(lane == ATT_HEAD_DIM + 2, lo, 0.0)))
        comp0 = jnp.where(first, tnorm, aug)
        comp1 = jnp.where(first, pltpu.roll(tnorm, ATT_HEAD_DIM, 1), aug)
        o_ref[:, 2 * c * LANES:(2 * c + 1) * LANES] = comp0.astype(BF16)
        o_ref[:, (2 * c + 1) * LANES:(2 * c + 2) * LANES] = comp1.astype(BF16)


def _qk_call(x, g_mix, w_in, gain_row, ones_row, slope_row, *, tm, tn, tk):
    s = x.shape[0]
    n_cols = 2 * ATT_QK_WIDTH
    return pl.pallas_call(
        functools.partial(_qk_kernel, tm=tm, tn=tn, tk=tk),
        grid=(s // tm, n_cols // tn),
        in_specs=[
            pl.BlockSpec((tm, D_MODEL), lambda i, j: (i, 0)),
            pl.BlockSpec((1, D_MODEL), lambda i, j: (0, 0)),
            pl.BlockSpec((D_MODEL, tn), lambda i, j: (0, j)),
            pl.BlockSpec((1, tn), lambda i, j: (0, j)),
            pl.BlockSpec((1, tn), lambda i, j: (0, j)),
            pl.BlockSpec((1, tn), lambda i, j: (0, j)),
        ],
        out_specs=pl.BlockSpec((tm, 2 * tn), lambda i, j: (i, j)),
        out_shape=jax.ShapeDtypeStruct((s, 2 * n_cols), BF16),
        scratch_shapes=[pltpu.VMEM((tm, D_MODEL), BF16)],
        compiler_params=_params("arbitrary", "arbitrary"),
        name="qk_proj",
    )(x, g_mix, w_in, gain_row, ones_row, slope_row)


def _v_kernel(x_ref, g_ref, w_ref, o_ref, h_ref):
    @pl.when(pl.program_id(1) == 0)
    def _():
        h_ref[...] = _rms_rows(x_ref[...], g_ref[...]).astype(BF16)

    o_ref[...] = jnp.dot(h_ref[...], w_ref[...], preferred_element_type=F32).astype(BF16)


def _v_call(x, g_mix, w_in, *, tm, tn):
    s = x.shape[0]
    off = OFF_V // tn
    return pl.pallas_call(
        _v_kernel,
        grid=(s // tm, ATT_WIDTH // tn),
        in_specs=[
            pl.BlockSpec((tm, D_MODEL), lambda i, j: (i, 0)),
            pl.BlockSpec((1, D_MODEL), lambda i, j: (0, 0)),
            pl.BlockSpec((D_MODEL, tn), lambda i, j: (0, off + j)),
        ],
        out_specs=pl.BlockSpec((tm, tn), lambda i, j: (i, j)),
        out_shape=jax.ShapeDtypeStruct((s, ATT_WIDTH), BF16),
        scratch_shapes=[pltpu.VMEM((tm, D_MODEL), BF16)],
        compiler_params=_params("arbitrary", "arbitrary"),
        name="v_proj",
    )(x, g_mix, w_in)


def _conv_kernel(x_ref, g_ref, wb_ref, wc_ref, wx_ref, cw_ref, o_ref, h_ref, carry_ref, *, tm):
    i = pl.program_id(0)
    j = pl.program_id(1)

    @pl.when(j == 0)
    def _():
        h_ref[...] = _rms_rows(x_ref[...], g_ref[...]).astype(BF16)

    @pl.when(i == 0)
    def _():
        carry_ref[j] = jnp.zeros(carry_ref.shape[1:], F32)

    h = h_ref[...]
    c_b = jnp.dot(h, wb_ref[...], preferred_element_type=F32)
    z = (jnp.dot(h, wc_ref[...], preferred_element_type=F32)
         * jnp.dot(h, wx_ref[...], preferred_element_type=F32))
    y = _causal_conv3(z, cw_ref[...], carry_ref[j])
    o_ref[...] = (c_b * y).astype(BF16)
    carry_ref[j] = z[tm - CARRY_ROWS:tm]


def _conv_call(x, g_mix, w_in, w_conv, *, tm, tn):
    s = x.shape[0]
    n_j = CONV_WIDTH // tn
    return pl.pallas_call(
        functools.partial(_conv_kernel, tm=tm),
        grid=(s // tm, n_j),
        in_specs=[
            pl.BlockSpec((tm, D_MODEL), lambda i, j: (i, 0)),
            pl.BlockSpec((1, D_MODEL), lambda i, j: (0, 0)),
            pl.BlockSpec((D_MODEL, tn), lambda i, j: (0, OFF_CB // tn + j)),
            pl.BlockSpec((D_MODEL, tn), lambda i, j: (0, OFF_CC // tn + j)),
            pl.BlockSpec((D_MODEL, tn), lambda i, j: (0, OFF_CX // tn + j)),
            pl.BlockSpec((3, tn), lambda i, j: (0, j)),
        ],
        out_specs=pl.BlockSpec((tm, tn), lambda i, j: (i, j)),
        out_shape=jax.ShapeDtypeStruct((s, CONV_WIDTH), BF16),
        scratch_shapes=[pltpu.VMEM((tm, D_MODEL), BF16),
                        pltpu.VMEM((n_j, CARRY_ROWS, tn), F32)],
        compiler_params=_params("arbitrary", "arbitrary"),
        name="conv_branch",
    )(x, g_mix, w_in, w_in, w_in, w_conv)


def _attn_kernel(slopes_ref, lam_ref, gsub_ref, q0_ref, q1_ref, k0_ref, k1_ref, v_ref, o_ref,
                 m_ref, l_ref, acc_ref, *, t, lambda_init):
    head = pl.program_id(0)
    i = pl.program_id(1)
    slope = slopes_ref[head]

    m_ref[...] = jnp.full(m_ref.shape, MASK_VALUE, F32)
    l_ref[...] = jnp.zeros(l_ref.shape, F32)
    acc_ref[...] = jnp.zeros(acc_ref.shape, F32)

    q = (q0_ref[...], q1_ref[...])
    k_refs = (k0_ref, k1_ref)

    def tile(j, masked):
        delta = slope * ((j - i) * t).astype(F32)
        start = pl.multiple_of(j * t, t)
        v = v_ref[pl.ds(start, t), :]
        if masked:
            keep = (lax.broadcasted_iota(jnp.int32, (t, t), 1)
                    <= lax.broadcasted_iota(jnp.int32, (t, t), 0))
        for c in range(2):
            k = k_refs[c][pl.ds(start, t), :]
            s = lax.dot_general(q[c], k, NT_DIMS, preferred_element_type=F32)
            if masked:
                s = jnp.where(keep, s, MASK_VALUE)
            m_prev = m_ref[c]
            m_new = jnp.maximum(m_prev, jnp.max(s, axis=1, keepdims=True) + delta)
            alpha = jnp.exp(m_prev - m_new)
            p = jnp.exp(s - (m_new - delta))
            l_ref[c] = alpha * l_ref[c] + jnp.sum(p, axis=1, keepdims=True)
            acc_ref[c] = alpha * acc_ref[c] + jnp.dot(p.astype(BF16), v, preferred_element_type=F32)
            m_ref[c] = m_new

    def body(j, carry):
        tile(j, False)
        return carry

    lax.fori_loop(0, i, body, 0)
    tile(i, True)

    lam_rows = lam_ref[...]
    lam = (jnp.exp(jnp.sum(lam_rows[0:1] * lam_rows[1:2], axis=-1, keepdims=True))
           - jnp.exp(jnp.sum(lam_rows[2:3] * lam_rows[3:4], axis=-1, keepdims=True))
           + lambda_init)
    o = acc_ref[0] / l_ref[0] - lam * (acc_ref[1] / l_ref[1])
    o_ref[...] = (_rms_rows(o, gsub_ref[...]) * (1.0 - lambda_init)).astype(BF16)


def _attn_call(slopes, lam_rows, g_sub, qk, v, *, t, lambda_init):
    s = v.shape[0]
    k_off = 2 * ATT_HEADS
    return pl.pallas_call(
        functools.partial(_attn_kernel, t=t, lambda_init=lambda_init),
        grid=(ATT_HEADS, s // t),
        in_specs=[
            pl.BlockSpec(memory_space=pltpu.SMEM),
            pl.BlockSpec((4, ATT_HEAD_DIM), lambda h, i: (0, 0)),
            pl.BlockSpec((1, ATT_V_DIM), lambda h, i: (0, 0)),
            pl.BlockSpec((t, LANES), lambda h, i: (i, 2 * h)),
            pl.BlockSpec((t, LANES), lambda h, i: (i, 2 * h + 1)),
            pl.BlockSpec((s, LANES), lambda h, i: (0, k_off + 2 * h)),
            pl.BlockSpec((s, LANES), lambda h, i: (0, k_off + 2 * h + 1)),
            pl.BlockSpec((s, ATT_V_DIM), lambda h, i: (0, h)),
        ],
        out_specs=pl.BlockSpec((t, ATT_V_DIM), lambda h, i: (i, h)),
        out_shape=jax.ShapeDtypeStruct((s, ATT_WIDTH), BF16),
        scratch_shapes=[pltpu.VMEM((2, t, 1), F32),
                        pltpu.VMEM((2, t, 1), F32),
                        pltpu.VMEM((2, t, ATT_V_DIM), F32)],
        compiler_params=_params("arbitrary", "arbitrary"),
        name="diff_attn",
    )(slopes, lam_rows, g_sub, qk, qk, qk, qk, v)


def _merge_kernel(x_ref, g_ref, ya_ref, yc_ref, wga_ref, wgc_ref, bg_ref, wa_ref, wc_ref, wo_ref,
                  gffn_ref, x1_ref, h2_ref):
    xf = x_ref[...]
    h = _rms_rows(xf, g_ref[...]).astype(BF16)
    bg = bg_ref[...]
    gate_a = jax.nn.sigmoid(jnp.dot(h, wga_ref[...], preferred_element_type=F32) + bg[0:1])
    merged = gate_a * jnp.dot(ya_ref[...], wa_ref[...], preferred_element_type=F32)
    gate_c = jax.nn.sigmoid(jnp.dot(h, wgc_ref[...], preferred_element_type=F32) + bg[1:2])
    merged = merged + gate_c * jnp.dot(yc_ref[...], wc_ref[...], preferred_element_type=F32)
    x1 = xf + jnp.dot(merged.astype(BF16), wo_ref[...], preferred_element_type=F32)
    x1_ref[...] = x1
    h2_ref[...] = _rms_rows(x1, gffn_ref[...]).astype(BF16)


def _merge_call(x, g_mix, y_att, y_conv, w_in, b_gates, w_att_out, w_conv_out, w_o, g_ffn, *, tm):
    s = x.shape[0]
    const = lambda i: (0, 0)
    return pl.pallas_call(
        _merge_kernel,
        grid=(s // tm,),
        in_specs=[
            pl.BlockSpec((tm, D_MODEL), lambda i: (i, 0)),
            _resident((1, D_MODEL), const),
            pl.BlockSpec((tm, ATT_WIDTH), lambda i: (i, 0)),
            pl.BlockSpec((tm, CONV_WIDTH), lambda i: (i, 0)),
            _resident((D_MODEL, D_MODEL), lambda i: (0, OFF_GA // D_MODEL)),
            _resident((D_MODEL, D_MODEL), lambda i: (0, OFF_GC // D_MODEL)),
            _resident((2, D_MODEL), const),
            _resident((ATT_WIDTH, D_MODEL), const),
            _resident((CONV_WIDTH, D_MODEL), const),
            _resident((D_MODEL, D_MODEL), const),
            _resident((1, D_MODEL), const),
        ],
        out_specs=[pl.BlockSpec((tm, D_MODEL), lambda i: (i, 0)),
                   pl.BlockSpec((tm, D_MODEL), lambda i: (i, 0))],
        out_shape=[jax.ShapeDtypeStruct((s, D_MODEL), F32),
                   jax.ShapeDtypeStruct((s, D_MODEL), BF16)],
        compiler_params=_params("arbitrary"),
        name="merge",
    )(x, g_mix, y_att, y_conv, w_in, w_in, b_gates, w_att_out, w_conv_out, w_o, g_ffn)


def _ffn_kernel(h_ref, x1_ref, wg_ref, wv_ref, cwg_ref, cwv_ref, wd_ref, o_ref, carry_ref, *, tm):
    i = pl.program_id(0)
    f = pl.program_id(1)

    @pl.when(i == 0)
    def _():
        carry_ref[f] = jnp.zeros(carry_ref.shape[1:], F32)

    h = h_ref[...]
    pre_g = jnp.dot(h, wg_ref[...], preferred_element_type=F32)
    pre_v = jnp.dot(h, wv_ref[...], preferred_element_type=F32)
    u_g = _causal_conv3(pre_g, cwg_ref[...], carry_ref[f, 0])
    u_v = _causal_conv3(pre_v, cwv_ref[...], carry_ref[f, 1])
    carry_ref[f, 0] = pre_g[tm - CARRY_ROWS:tm]
    carry_ref[f, 1] = pre_v[tm - CARRY_ROWS:tm]
    act = (u_g * jax.nn.sigmoid(u_g) * u_v).astype(BF16)
    contrib = jnp.dot(act, wd_ref[...], preferred_element_type=F32)

    @pl.when(f == 0)
    def _():
        o_ref[...] = x1_ref[...] + contrib

    @pl.when(f > 0)
    def _():
        o_ref[...] += contrib


def _ffn_call(h2, x1, w_up, w_ffn_conv, w_down, *, tm, tf):
    s = h2.shape[0]
    n_f = D_FF // tf
    return pl.pallas_call(
        functools.partial(_ffn_kernel, tm=tm),
        grid=(s // tm, n_f),
        in_specs=[
            pl.BlockSpec((tm, D_MODEL), lambda i, f: (i, 0)),
            pl.BlockSpec((tm, D_MODEL), lambda i, f: (i, 0)),
            pl.BlockSpec((D_MODEL, tf), lambda i, f: (0, f)),
            pl.BlockSpec((D_MODEL, tf), lambda i, f: (0, n_f + f)),
            pl.BlockSpec((3, tf), lambda i, f: (0, f)),
            pl.BlockSpec((3, tf), lambda i, f: (0, n_f + f)),
            pl.BlockSpec((tf, D_MODEL), lambda i, f: (f, 0)),
        ],
        out_specs=pl.BlockSpec((tm, D_MODEL), lambda i, f: (i, 0)),
        out_shape=jax.ShapeDtypeStruct((s, D_MODEL), F32),
        scratch_shapes=[pltpu.VMEM((n_f, 2, CARRY_ROWS, tf), F32)],
        compiler_params=_params("arbitrary", "arbitrary"),
        name="conv_ffn",
    )(h2, x1, w_up, w_up, w_ffn_conv, w_ffn_conv, w_down)


def _ple_kernel(x_ref, p_ref, wple_ref, gple_ref, gpg_ref, wpg_ref, bpg_ref, o_ref):
    xf = x_ref[...]
    pe = _rms_rows(jnp.dot(p_ref[...].astype(BF16), wple_ref[...], preferred_element_type=F32),
                   gple_ref[...])
    hn = _rms_rows(xf, gpg_ref[...]).astype(BF16)
    pg = jax.nn.sigmoid(jnp.dot(hn, wpg_ref[...], preferred_element_type=F32) + bpg_ref[...])
    o_ref[...] = xf + pg * pe


def _ple_call(x2, p, w_ple, g_ple, g_pg, w_pg, b_pg, *, tm):
    s = x2.shape[0]
    const = lambda i: (0, 0)
    return pl.pallas_call(
        _ple_kernel,
        grid=(s // tm,),
        in_specs=[
            pl.BlockSpec((tm, D_MODEL), lambda i: (i, 0)),
            pl.BlockSpec((tm, PLE_DIM), lambda i: (i, 0)),
            _resident((PLE_DIM, D_MODEL), const),
            _resident((1, D_MODEL), const),
            _resident((1, D_MODEL), const),
            _resident((D_MODEL, D_MODEL), const),
            _resident((1, D_MODEL), const),
        ],
        out_specs=pl.BlockSpec((tm, D_MODEL), lambda i: (i, 0)),
        out_shape=jax.ShapeDtypeStruct((s, D_MODEL), F32),
        compiler_params=_params("arbitrary"),
        name="ple_gate",
    )(x2, p, w_ple, g_ple, g_pg, w_pg, b_pg)


def _layer(depth_index, x, p, g_mix, w_in, b_gates, g_q, g_k, lam_rows, g_sub, w_conv_mix, w_att_out,
           w_conv_out, w_o, g_ffn, w_up, w_ffn_conv, w_down, w_ple, g_ple, g_pg, w_pg, b_pg):
    t_attn = 512
    lambda_init = 0.8 - 0.6 * math.exp(-0.3 * depth_index)
    row = lambda a: a.reshape(1, -1).astype(F32)
    n_groups = ATT_QK_WIDTH // ATT_HEAD_DIM
    slopes = jnp.asarray(2.0 ** (-8.0 * jnp.arange(1, ATT_HEADS + 1) / ATT_HEADS), F32)
    gain_row = jnp.concatenate([jnp.tile(g_q.astype(F32) * ATT_HEAD_DIM ** -0.5, n_groups),
                                jnp.tile(g_k.astype(F32), n_groups)]).reshape(1, -1)
    lane_in_group = jnp.arange(LANES) - ATT_HEAD_DIM
    bias_lanes = ((lane_in_group >= 0) & (lane_in_group < 3)).astype(F32)
    ones_row = jnp.concatenate([jnp.tile(bias_lanes, ATT_HEADS),
                                jnp.zeros(ATT_QK_WIDTH, F32)]).reshape(1, -1)
    slope_row = jnp.concatenate([jnp.zeros(ATT_QK_WIDTH, F32),
                                 jnp.repeat(slopes, 2 * ATT_HEAD_DIM)]).reshape(1, -1)

    w_in = w_in.astype(BF16)
    g_mix = row(g_mix)
    qk = _qk_call(x, g_mix, w_in, gain_row, ones_row, slope_row, tm=512, tn=256, tk=t_attn)
    v = _v_call(x, g_mix, w_in, tm=512, tn=512)
    y_conv = _conv_call(x, g_mix, w_in, w_conv_mix.astype(F32), tm=512, tn=256)
    y_att = _attn_call(slopes, lam_rows, row(g_sub), qk, v, t=t_attn, lambda_init=lambda_init)
    x1, h2 = _merge_call(x, g_mix, y_att, y_conv, w_in, b_gates.astype(F32),
                         w_att_out.astype(BF16), w_conv_out.astype(BF16), w_o.astype(BF16),
                         row(g_ffn), tm=256)
    x2 = _ffn_call(h2, x1, w_up.astype(BF16), w_ffn_conv.astype(F32), w_down.astype(BF16),
                   tm=512, tf=512)
    return _ple_call(x2, p, w_ple.astype(BF16), row(g_ple), row(g_pg), w_pg.astype(BF16),
                     row(b_pg), tm=256)


def kernel(x, p, g_mix, w_in, b_gates, g_q, g_k, lam_q1, lam_k1, lam_q2, lam_k2, g_sub, w_conv_mix,
           w_att_out, w_conv_out, w_o, g_ffn, w_up, w_ffn_conv, w_down, w_ple, g_ple, g_pg, w_pg,
           b_pg):
    batch, seq, d_model = x.shape
    depth = p.shape[0]
    outs = []
    for b in range(batch):
        xb = x.reshape(seq, d_model) if batch == 1 else x[b]
        for i in range(depth):
            lam_rows = jnp.stack([lam_q1[i], lam_k1[i], lam_q2[i], lam_k2[i]]).astype(F32)
            pb = p.reshape(seq, PLE_DIM) if batch == 1 and depth == 1 else p[i, b]
            xb = _layer(i, xb, pb, g_mix[i], w_in[i], b_gates[i], g_q[i], g_k[i], lam_rows,
                        g_sub[i], w_conv_mix[i], w_att_out[i], w_conv_out[i], w_o[i], g_ffn[i],
                        w_up[i], w_ffn_conv[i], w_down[i], w_ple[i], g_ple[i], g_pg[i], w_pg[i],
                        b_pg[i])
        outs.append(xb)
    return outs[0].reshape(x.shape) if batch == 1 else jnp.stack(outs)
```

```python
import functools
import math

import jax
import jax.numpy as jnp
from jax import lax
from jax.experimental import pallas as pl
from jax.experimental.pallas import tpu as pltpu

F32 = jnp.float32
BF16 = jnp.bfloat16

D_MODEL = 2048
ATT_HEADS = 8
ATT_HEAD_DIM = 64
ATT_V_DIM = 2 * ATT_HEAD_DIM
ATT_QK_WIDTH = ATT_HEADS * 2 * ATT_HEAD_DIM
ATT_WIDTH = ATT_HEADS * ATT_V_DIM
CONV_WIDTH = 1024
D_FF = 5632
PLE_DIM = 256
EPS = 1e-6

LANES = 128
CARRY_ROWS = 8
MASK_VALUE = -1e30
VMEM_LIMIT = 56 * 1024 * 1024

OFF_Q = 0
OFF_K = ATT_QK_WIDTH
OFF_V = 2 * ATT_QK_WIDTH
OFF_CB = OFF_V + ATT_WIDTH
OFF_CC = OFF_CB + CONV_WIDTH
OFF_CX = OFF_CC + CONV_WIDTH
OFF_GA = OFF_CX + CONV_WIDTH
OFF_GC = OFF_GA + D_MODEL

NT_DIMS = (((1,), (1,)), ((), ()))


def _rms_rows(xf, g):
    ms = jnp.mean(xf * xf, axis=-1, keepdims=True)
    return xf * lax.rsqrt(ms + EPS) * g


def _params(*sem):
    return pltpu.CompilerParams(dimension_semantics=sem, vmem_limit_bytes=VMEM_LIMIT)


def _resident(shape, index_map):
    return pl.BlockSpec(shape, index_map, pipeline_mode=pl.Buffered(1))


def _shift_rows(u, prev, rows):
    u1 = jnp.where(rows == 0, prev[CARRY_ROWS - 1:CARRY_ROWS], pltpu.roll(u, 1, 0))
    u2 = jnp.where(rows == 0, prev[CARRY_ROWS - 2:CARRY_ROWS - 1],
                   jnp.where(rows == 1, prev[CARRY_ROWS - 1:CARRY_ROWS], pltpu.roll(u, 2, 0)))
    return u1, u2


def _causal_conv3(u, w, prev):
    rows = lax.broadcasted_iota(jnp.int32, u.shape, 0)
    u1, u2 = _shift_rows(u, prev, rows)
    return w[0:1] * u2 + w[1:2] * u1 + w[2:3] * u


ROW_GAIN, ROW_ONES, ROW_SLOPE, ROW_HI, ROW_MID, ROW_LO, TABLE_ROWS = 0, 1, 2, 3, 4, 5, 8


def _qk_kernel(x_ref, g_ref, w_ref, tab_ref, o_ref, h_ref, *, tm, tn):
    i = pl.program_id(0)

    @pl.when(pl.program_id(1) == 0)
    def _():
        h_ref[...] = _rms_rows(x_ref[...], g_ref[...]).astype(BF16)

    acc = jnp.dot(h_ref[...], w_ref[...], preferred_element_type=F32)
    lane = lax.broadcasted_iota(jnp.int32, (tm, LANES), 1)
    pos = (i * tm + lax.broadcasted_iota(jnp.int32, (tm, LANES), 0)).astype(F32)
    first = lane < ATT_HEAD_DIM
    for c in range(tn // LANES):
        sl = slice(c * LANES, (c + 1) * LANES)
        t = acc[:, sl]
        sq = t * t
        ss0 = jnp.sum(jnp.where(first, sq, 0.0), axis=-1, keepdims=True)
        ss1 = jnp.sum(jnp.where(first, 0.0, sq), axis=-1, keepdims=True)
        r = jnp.where(first, lax.rsqrt(ss0 / ATT_HEAD_DIM + EPS), lax.rsqrt(ss1 / ATT_HEAD_DIM + EPS))
        tnorm = t * r * tab_ref[ROW_GAIN:ROW_GAIN + 1, sl]
        bias = tab_ref[ROW_SLOPE:ROW_SLOPE + 1, sl] * pos
        hi = bias.astype(BF16).astype(F32)
        rem = bias - hi
        mid = rem.astype(BF16).astype(F32)
        lo = rem - mid
        aug = (tab_ref[ROW_ONES:ROW_ONES + 1, sl] + tab_ref[ROW_HI:ROW_HI + 1, sl] * hi
               + tab_ref[ROW_MID:ROW_MID + 1, sl] * mid + tab_ref[ROW_LO:ROW_LO + 1, sl] * lo)
        comp0 = jnp.where(first, tnorm, aug)
        comp1 = jnp.where(first, pltpu.roll(tnorm, ATT_HEAD_DIM, 1), aug)
        o_ref[:, 2 * c * LANES:(2 * c + 1) * LANES] = comp0.astype(BF16)
        o_ref[:, (2 * c + 1) * LANES:(2 * c + 2) * LANES] = comp1.astype(BF16)


def _qk_table(g_q, g_k):
    n_groups = ATT_QK_WIDTH // ATT_HEAD_DIM
    slopes = 2.0 ** (-8.0 * jnp.arange(1, ATT_HEADS + 1, dtype=F32) / ATT_HEADS)
    slope_cols = jnp.repeat(slopes, 2 * ATT_HEAD_DIM)
    lane = jnp.tile(jnp.arange(LANES), ATT_HEADS) - ATT_HEAD_DIM
    at = lambda n: (lane == n).astype(F32)
    zeros = jnp.zeros(ATT_QK_WIDTH, F32)
    both = lambda q_part, k_part: jnp.concatenate([q_part, k_part])
    rows = [None] * TABLE_ROWS
    rows[ROW_GAIN] = both(jnp.tile(g_q.astype(F32) * ATT_HEAD_DIM ** -0.5, n_groups),
                          jnp.tile(g_k.astype(F32), n_groups))
    rows[ROW_ONES] = both(at(0) + at(1) + at(2), at(3) + at(4) + at(5))
    rows[ROW_SLOPE] = both(-slope_cols, slope_cols)
    rows[ROW_HI] = both(at(3), at(0))
    rows[ROW_MID] = both(at(4), at(1))
    rows[ROW_LO] = both(at(5), at(2))
    return jnp.stack([both(zeros, zeros) if r is None else r for r in rows])


def _qk_call(x, g_mix, w_in, table, *, tm, tn):
    s = x.shape[0]
    n_cols = 2 * ATT_QK_WIDTH
    return pl.pallas_call(
        functools.partial(_qk_kernel, tm=tm, tn=tn),
        grid=(s // tm, n_cols // tn),
        in_specs=[
            pl.BlockSpec((tm, D_MODEL), lambda i, j: (i, 0)),
            pl.BlockSpec((1, D_MODEL), lambda i, j: (0, 0)),
            pl.BlockSpec((D_MODEL, tn), lambda i, j: (0, j)),
            pl.BlockSpec((TABLE_ROWS, tn), lambda i, j: (0, j)),
        ],
        out_specs=pl.BlockSpec((tm, 2 * tn), lambda i, j: (i, j)),
        out_shape=jax.ShapeDtypeStruct((s, 2 * n_cols), BF16),
        scratch_shapes=[pltpu.VMEM((tm, D_MODEL), BF16)],
        compiler_params=_params("arbitrary", "arbitrary"),
        name="qk_proj",
    )(x, g_mix, w_in, table)


def _v_kernel(x_ref, g_ref, w_ref, o_ref, h_ref):
    @pl.when(pl.program_id(1) == 0)
    def _():
        h_ref[...] = _rms_rows(x_ref[...], g_ref[...]).astype(BF16)

    o_ref[...] = jnp.dot(h_ref[...], w_ref[...], preferred_element_type=F32).astype(BF16)


def _v_call(x, g_mix, w_in, *, tm, tn):
    s = x.shape[0]
    off = OFF_V // tn
    return pl.pallas_call(
        _v_kernel,
        grid=(s // tm, ATT_WIDTH // tn),
        in_specs=[
            pl.BlockSpec((tm, D_MODEL), lambda i, j: (i, 0)),
            pl.BlockSpec((1, D_MODEL), lambda i, j: (0, 0)),
            pl.BlockSpec((D_MODEL, tn), lambda i, j: (0, off + j)),
        ],
        out_specs=pl.BlockSpec((tm, tn), lambda i, j: (i, j)),
        out_shape=jax.ShapeDtypeStruct((s, ATT_WIDTH), BF16),
        scratch_shapes=[pltpu.VMEM((tm, D_MODEL), BF16)],
        compiler_params=_params("arbitrary", "arbitrary"),
        name="v_proj",
    )(x, g_mix, w_in)


def _conv_kernel(x_ref, g_ref, wb_ref, wc_ref, wx_ref, cw_ref, o_ref, h_ref, carry_ref, *, tm):
    i = pl.program_id(0)
    j = pl.program_id(1)

    @pl.when(j == 0)
    def _():
        h_ref[...] = _rms_rows(x_ref[...], g_ref[...]).astype(BF16)

    @pl.when(i == 0)
    def _():
        carry_ref[j] = jnp.zeros(carry_ref.shape[1:], F32)

    h = h_ref[...]
    c_b = jnp.dot(h, wb_ref[...], preferred_element_type=F32)
    z = (jnp.dot(h, wc_ref[...], preferred_element_type=F32)
         * jnp.dot(h, wx_ref[...], preferred_element_type=F32))
    y = _causal_conv3(z, cw_ref[...], carry_ref[j])
    o_ref[...] = (c_b * y).astype(BF16)
    carry_ref[j] = z[tm - CARRY_ROWS:tm]


def _conv_call(x, g_mix, w_in, w_conv, *, tm, tn):
    s = x.shape[0]
    n_j = CONV_WIDTH // tn
    return pl.pallas_call(
        functools.partial(_conv_kernel, tm=tm),
        grid=(s // tm, n_j),
        in_specs=[
            pl.BlockSpec((tm, D_MODEL), lambda i, j: (i, 0)),
            pl.BlockSpec((1, D_MODEL), lambda i, j: (0, 0)),
            pl.BlockSpec((D_MODEL, tn), lambda i, j: (0, OFF_CB // tn + j)),
            pl.BlockSpec((D_MODEL, tn), lambda i, j: (0, OFF_CC // tn + j)),
            pl.BlockSpec((D_MODEL, tn), lambda i, j: (0, OFF_CX // tn + j)),
            pl.BlockSpec((3, tn), lambda i, j: (0, j)),
        ],
        out_specs=pl.BlockSpec((tm, tn), lambda i, j: (i, j)),
        out_shape=jax.ShapeDtypeStruct((s, CONV_WIDTH), BF16),
        scratch_shapes=[pltpu.VMEM((tm, D_MODEL), BF16),
                        pltpu.VMEM((n_j, CARRY_ROWS, tn), F32)],
        compiler_params=_params("arbitrary", "arbitrary"),
        name="conv_branch",
    )(x, g_mix, w_in, w_in, w_in, w_conv)


def _attn_kernel(lam_ref, gsub_ref, q0_ref, q1_ref, k0_ref, k1_ref, v_ref, o_ref,
                 m_ref, l_ref, acc_ref, *, t, lambda_init):
    i = pl.program_id(1)
    m_ref[...] = jnp.full(m_ref.shape, MASK_VALUE, F32)
    l_ref[...] = jnp.zeros(l_ref.shape, F32)
    acc_ref[...] = jnp.zeros(acc_ref.shape, F32)

    q = (q0_ref[...], q1_ref[...])
    k_refs = (k0_ref, k1_ref)
    n_chunks = t // LANES

    def tile(j, masked):
        start = pl.multiple_of(j * t, t)
        v = v_ref[pl.ds(start, t), :]
        if masked:
            keep = (lax.broadcasted_iota(jnp.int32, (t, t), 1)
                    <= lax.broadcasted_iota(jnp.int32, (t, t), 0))
        for c in range(2):
            k = k_refs[c][pl.ds(start, t), :]
            s = lax.dot_general(q[c], k, NT_DIMS, preferred_element_type=F32)
            if masked:
                s = jnp.where(keep, s, MASK_VALUE)
            chunks = [s[:, a * LANES:(a + 1) * LANES] for a in range(n_chunks)]
            m_prev = m_ref[c]
            m_new = jnp.maximum(m_prev, jnp.max(functools.reduce(jnp.maximum, chunks),
                                                axis=1, keepdims=True))
            alpha = jnp.exp(m_prev - m_new)
            ps = [jnp.exp(ch - m_new) for ch in chunks]
            l_ref[c] = alpha * l_ref[c] + functools.reduce(jnp.add, ps)
            p = jnp.concatenate([x.astype(BF16) for x in ps], axis=1)
            acc_ref[c] = alpha * acc_ref[c] + jnp.dot(p, v, preferred_element_type=F32)
            m_ref[c] = m_new

    def body(j, carry):
        tile(j, False)
        return carry

    lax.fori_loop(0, i, body, 0)
    tile(i, True)

    lam_rows = lam_ref[...]
    lam = (jnp.exp(jnp.sum(lam_rows[0:1] * lam_rows[1:2], axis=-1, keepdims=True))
           - jnp.exp(jnp.sum(lam_rows[2:3] * lam_rows[3:4], axis=-1, keepdims=True))
           + lambda_init)
    l0 = jnp.sum(l_ref[0], axis=1, keepdims=True)
    l1 = jnp.sum(l_ref[1], axis=1, keepdims=True)
    o = acc_ref[0] / l0 - lam * (acc_ref[1] / l1)
    o_ref[...] = (_rms_rows(o, gsub_ref[...]) * (1.0 - lambda_init)).astype(BF16)


def _attn_call(lam_rows, g_sub, qk, v, *, t, lambda_init):
    s = v.shape[0]
    k_off = 2 * ATT_HEADS
    return pl.pallas_call(
        functools.partial(_attn_kernel, t=t, lambda_init=lambda_init),
        grid=(ATT_HEADS, s // t),
        in_specs=[
            pl.BlockSpec((4, ATT_HEAD_DIM), lambda h, i: (0, 0)),
            pl.BlockSpec((1, ATT_V_DIM), lambda h, i: (0, 0)),
            pl.BlockSpec((t, LANES), lambda h, i: (i, 2 * h)),
            pl.BlockSpec((t, LANES), lambda h, i: (i, 2 * h + 1)),
            pl.BlockSpec((s, LANES), lambda h, i: (0, k_off + 2 * h)),
            pl.BlockSpec((s, LANES), lambda h, i: (0, k_off + 2 * h + 1)),
            pl.BlockSpec((s, ATT_V_DIM), lambda h, i: (0, h)),
        ],
        out_specs=pl.BlockSpec((t, ATT_V_DIM), lambda h, i: (i, h)),
        out_shape=jax.ShapeDtypeStruct((s, ATT_WIDTH), BF16),
        scratch_shapes=[pltpu.VMEM((2, t, LANES), F32),
                        pltpu.VMEM((2, t, LANES), F32),
                        pltpu.VMEM((2, t, ATT_V_DIM), F32)],
        compiler_params=_params("arbitrary", "arbitrary"),
        name="diff_attn",
    )(lam_rows, g_sub, qk, qk, qk, qk, v)


def _merge_kernel(x_ref, g_ref, ya_ref, yc_ref, wga_ref, wgc_ref, bg_ref, wa_ref, wc_ref, wo_ref,
                  gffn_ref, x1_ref, h2_ref):
    xf = x_ref[...]
    h = _rms_rows(xf, g_ref[...]).astype(BF16)
    bg = bg_ref[...]
    gate_a = jax.nn.sigmoid(jnp.dot(h, wga_ref[...], preferred_element_type=F32) + bg[0:1])
    merged = gate_a * jnp.dot(ya_ref[...], wa_ref[...], preferred_element_type=F32)
    gate_c = jax.nn.sigmoid(jnp.dot(h, wgc_ref[...], preferred_element_type=F32) + bg[1:2])
    merged = merged + gate_c * jnp.dot(yc_ref[...], wc_ref[...], preferred_element_type=F32)
    x1 = xf + jnp.dot(merged.astype(BF16), wo_ref[...], preferred_element_type=F32)
    x1_ref[...] = x1
    h2_ref[...] = _rms_rows(x1, gffn_ref[...]).astype(BF16)


def _merge_call(x, g_mix, y_att, y_conv, w_in, b_gates, w_att_out, w_conv_out, w_o, g_ffn, *, tm):
    s = x.shape[0]
    const = lambda i: (0, 0)
    return pl.pallas_call(
        _merge_kernel,
        grid=(s // tm,),
        in_specs=[
            pl.BlockSpec((tm, D_MODEL), lambda i: (i, 0)),
            _resident((1, D_MODEL), const),
            pl.BlockSpec((tm, ATT_WIDTH), lambda i: (i, 0)),
            pl.BlockSpec((tm, CONV_WIDTH), lambda i: (i, 0)),
            _resident((D_MODEL, D_MODEL), lambda i: (0, OFF_GA // D_MODEL)),
            _resident((D_MODEL, D_MODEL), lambda i: (0, OFF_GC // D_MODEL)),
            _resident((2, D_MODEL), const),
            _resident((ATT_WIDTH, D_MODEL), const),
            _resident((CONV_WIDTH, D_MODEL), const),
            _resident((D_MODEL, D_MODEL), const),
            _resident((1, D_MODEL), const),
        ],
        out_specs=[pl.BlockSpec((tm, D_MODEL), lambda i: (i, 0)),
                   pl.BlockSpec((tm, D_MODEL), lambda i: (i, 0))],
        out_shape=[jax.ShapeDtypeStruct((s, D_MODEL), F32),
                   jax.ShapeDtypeStruct((s, D_MODEL), BF16)],
        compiler_params=_params("arbitrary"),
        name="merge",
    )(x, g_mix, y_att, y_conv, w_in, w_in, b_gates, w_att_out, w_conv_out, w_o, g_ffn)


def _ffn_kernel(h_ref, x1_ref, wg_ref, wv_ref, cwg_ref, cwv_ref, wd_ref, o_ref, carry_ref, *, tm):
    i = pl.program_id(0)
    f = pl.program_id(1)

    @pl.when(i == 0)
    def _():
        carry_ref[f] = jnp.zeros(carry_ref.shape[1:], F32)

    h = h_ref[...]
    pre_g = jnp.dot(h, wg_ref[...], preferred_element_type=F32)
    pre_v = jnp.dot(h, wv_ref[...], preferred_element_type=F32)
    u_g = _causal_conv3(pre_g, cwg_ref[...], carry_ref[f, 0])
    u_v = _causal_conv3(pre_v, cwv_ref[...], carry_ref[f, 1])
    carry_ref[f, 0] = pre_g[tm - CARRY_ROWS:tm]
    carry_ref[f, 1] = pre_v[tm - CARRY_ROWS:tm]
    act = (u_g * jax.nn.sigmoid(u_g) * u_v).astype(BF16)
    contrib = jnp.dot(act, wd_ref[...], preferred_element_type=F32)

    @pl.when(f == 0)
    def _():
        o_ref[...] = x1_ref[...] + contrib

    @pl.when(f > 0)
    def _():
        o_ref[...] += contrib


def _ffn_call(h2, x1, w_up, w_ffn_conv, w_down, *, tm, tf):
    s = h2.shape[0]
    n_f = D_FF // tf
    return pl.pallas_call(
        functools.partial(_ffn_kernel, tm=tm),
        grid=(s // tm, n_f),
        in_specs=[
            pl.BlockSpec((tm, D_MODEL), lambda i, f: (i, 0)),
            pl.BlockSpec((tm, D_MODEL), lambda i, f: (i, 0)),
            pl.BlockSpec((D_MODEL, tf), lambda i, f: (0, f)),
            pl.BlockSpec((D_MODEL, tf), lambda i, f: (0, n_f + f)),
            pl.BlockSpec((3, tf), lambda i, f: (0, f)),
            pl.BlockSpec((3, tf), lambda i, f: (0, n_f + f)),
            pl.BlockSpec((tf, D_MODEL), lambda i, f: (f, 0)),
        ],
        out_specs=pl.BlockSpec((tm, D_MODEL), lambda i, f: (i, 0)),
        out_shape=jax.ShapeDtypeStruct((s, D_MODEL), F32),
        scratch_shapes=[pltpu.VMEM((n_f, 2, CARRY_ROWS, tf), F32)],
        compiler_params=_params("arbitrary", "arbitrary"),
        name="conv_ffn",
    )(h2, x1, w_up, w_up, w_ffn_conv, w_ffn_conv, w_down)


def _ple_kernel(x_ref, p_ref, wple_ref, gple_ref, gpg_ref, wpg_ref, bpg_ref, o_ref):
    xf = x_ref[...]
    pe = _rms_rows(jnp.dot(p_ref[...].astype(BF16), wple_ref[...], preferred_element_type=F32),
                   gple_ref[...])
    hn = _rms_rows(xf, gpg_ref[...]).astype(BF16)
    pg = jax.nn.sigmoid(jnp.dot(hn, wpg_ref[...], preferred_element_type=F32) + bpg_ref[...])
    o_ref[...] = xf + pg * pe


def _ple_call(x2, p, w_ple, g_ple, g_pg, w_pg, b_pg, *, tm):
    s = x2.shape[0]
    const = lambda i: (0, 0)
    return pl.pallas_call(
        _ple_kernel,
        grid=(s // tm,),
        in_specs=[
            pl.BlockSpec((tm, D_MODEL), lambda i: (i, 0)),
            pl.BlockSpec((tm, PLE_DIM), lambda i: (i, 0)),
            _resident((PLE_DIM, D_MODEL), const),
            _resident((1, D_MODEL), const),
            _resident((1, D_MODEL), const),
            _resident((D_MODEL, D_MODEL), const),
            _resident((1, D_MODEL), const),
        ],
        out_specs=pl.BlockSpec((tm, D_MODEL), lambda i: (i, 0)),
        out_shape=jax.ShapeDtypeStruct((s, D_MODEL), F32),
        compiler_params=_params("arbitrary"),
        name="ple_gate",
    )(x2, p, w_ple, g_ple, g_pg, w_pg, b_pg)


def _layer(depth_index, x, p, g_mix, w_in, b_gates, g_q, g_k, lam_rows, g_sub, w_conv_mix, w_att_out,
           w_conv_out, w_o, g_ffn, w_up, w_ffn_conv, w_down, w_ple, g_ple, g_pg, w_pg, b_pg):
    lambda_init = 0.8 - 0.6 * math.exp(-0.3 * depth_index)
    row = lambda a: a.reshape(1, -1).astype(F32)

    w_in = w_in.astype(BF16)
    g_mix = row(g_mix)
    qk = _qk_call(x, g_mix, w_in, _qk_table(g_q, g_k), tm=512, tn=256)
    v = _v_call(x, g_mix, w_in, tm=512, tn=512)
    y_conv = _conv_call(x, g_mix, w_in, w_conv_mix.astype(F32), tm=512, tn=256)
    y_att = _attn_call(lam_rows, row(g_sub), qk, v, t=512, lambda_init=lambda_init)
    x1, h2 = _merge_call(x, g_mix, y_att, y_conv, w_in, b_gates.astype(F32),
                         w_att_out.astype(BF16), w_conv_out.astype(BF16), w_o.astype(BF16),
                         row(g_ffn), tm=256)
    x2 = _ffn_call(h2, x1, w_up.astype(BF16), w_ffn_conv.astype(F32), w_down.astype(BF16),
                   tm=512, tf=512)
    return _ple_call(x2, p, w_ple.astype(BF16), row(g_ple), row(g_pg), w_pg.astype(BF16),
                     row(b_pg), tm=256)


def kernel(x, p, g_mix, w_in, b_gates, g_q, g_k, lam_q1, lam_k1, lam_q2, lam_k2, g_sub, w_conv_mix,
           w_att_out, w_conv_out, w_o, g_ffn, w_up, w_ffn_conv, w_down, w_ple, g_ple, g_pg, w_pg,
           b_pg):
    batch, seq, d_model = x.shape
    depth = p.shape[0]
    outs = []
    for b in range(batch):
        xb = x.reshape(seq, d_model) if batch == 1 else x[b]
        for i in range(depth):
            lam_rows = jnp.stack([lam_q1[i], lam_k1[i], lam_q2[i], lam_k2[i]]).astype(F32)
            pb = p.reshape(seq, PLE_DIM) if batch == 1 and depth == 1 else p[i, b]
            xb = _layer(i, xb, pb, g_mix[i], w_in[i], b_gates[i], g_q[i], g_k[i], lam_rows,
                        g_sub[i], w_conv_mix[i], w_att_out[i], w_conv_out[i], w_o[i], g_ffn[i],
                        w_up[i], w_ffn_conv[i], w_down[i], w_ple[i], g_ple[i], g_pg[i], w_pg[i],
                        b_pg[i])
        outs.append(xb)
    return outs[0].reshape(x.shape) if batch == 1 else jnp.stack(outs)
```

```python
import functools
import math

import jax
import jax.numpy as jnp
from jax import lax
from jax.experimental import pallas as pl
from jax.experimental.pallas import tpu as pltpu

F32 = jnp.float32
BF16 = jnp.bfloat16

D_MODEL = 2048
ATT_HEADS = 8
ATT_HEAD_DIM = 64
ATT_V_DIM = 2 * ATT_HEAD_DIM
ATT_QK_WIDTH = ATT_HEADS * 2 * ATT_HEAD_DIM
ATT_WIDTH = ATT_HEADS * ATT_V_DIM
CONV_WIDTH = 1024
D_FF = 5632
PLE_DIM = 256
EPS = 1e-6

LANES = 128
CARRY_ROWS = 8
MASK_VALUE = -1e30
VMEM_LIMIT = 56 * 1024 * 1024

OFF_Q = 0
OFF_K = ATT_QK_WIDTH
OFF_V = 2 * ATT_QK_WIDTH
OFF_CB = OFF_V + ATT_WIDTH
OFF_CC = OFF_CB + CONV_WIDTH
OFF_CX = OFF_CC + CONV_WIDTH
OFF_GA = OFF_CX + CONV_WIDTH
OFF_GC = OFF_GA + D_MODEL

NT_DIMS = (((1,), (1,)), ((), ()))


def _rms_rows(xf, g):
    ms = jnp.mean(xf * xf, axis=-1, keepdims=True)
    return xf * lax.rsqrt(ms + EPS) * g


def _params(*sem):
    return pltpu.CompilerParams(dimension_semantics=sem, vmem_limit_bytes=VMEM_LIMIT)


def _resident(shape, index_map):
    return pl.BlockSpec(shape, index_map, pipeline_mode=pl.Buffered(1))


def _shift_rows(u, prev, rows):
    u1 = jnp.where(rows == 0, prev[CARRY_ROWS - 1:CARRY_ROWS], pltpu.roll(u, 1, 0))
    u2 = jnp.where(rows == 0, prev[CARRY_ROWS - 2:CARRY_ROWS - 1],
                   jnp.where(rows == 1, prev[CARRY_ROWS - 1:CARRY_ROWS], pltpu.roll(u, 2, 0)))
    return u1, u2


def _causal_conv3(u, w, prev):
    rows = lax.broadcasted_iota(jnp.int32, u.shape, 0)
    u1, u2 = _shift_rows(u, prev, rows)
    return w[0:1] * u2 + w[1:2] * u1 + w[2:3] * u


ROW_GAIN, ROW_ONES, ROW_SLOPE, ROW_HI, ROW_MID, ROW_LO, TABLE_ROWS = 0, 1, 2, 3, 4, 5, 8


def _qk_kernel(x_ref, g_ref, w_ref, tab_ref, o_ref, h_ref, *, tm, tn):
    i = pl.program_id(0)

    @pl.when(pl.program_id(1) == 0)
    def _():
        h_ref[...] = _rms_rows(x_ref[...], g_ref[...]).astype(BF16)

    acc = jnp.dot(h_ref[...], w_ref[...], preferred_element_type=F32)
    lane = lax.broadcasted_iota(jnp.int32, (tm, LANES), 1)
    pos = (i * tm + lax.broadcasted_iota(jnp.int32, (tm, LANES), 0)).astype(F32)
    first = lane < ATT_HEAD_DIM
    for c in range(tn // LANES):
        sl = slice(c * LANES, (c + 1) * LANES)
        t = acc[:, sl]
        sq = t * t
        ss0 = jnp.sum(jnp.where(first, sq, 0.0), axis=-1, keepdims=True)
        ss1 = jnp.sum(jnp.where(first, 0.0, sq), axis=-1, keepdims=True)
        r = jnp.where(first, lax.rsqrt(ss0 / ATT_HEAD_DIM + EPS), lax.rsqrt(ss1 / ATT_HEAD_DIM + EPS))
        tnorm = t * r * tab_ref[ROW_GAIN:ROW_GAIN + 1, sl]
        bias = tab_ref[ROW_SLOPE:ROW_SLOPE + 1, sl] * pos
        hi = bias.astype(BF16).astype(F32)
        rem = bias - hi
        mid = rem.astype(BF16).astype(F32)
        lo = rem - mid
        aug = (tab_ref[ROW_ONES:ROW_ONES + 1, sl] + tab_ref[ROW_HI:ROW_HI + 1, sl] * hi
               + tab_ref[ROW_MID:ROW_MID + 1, sl] * mid + tab_ref[ROW_LO:ROW_LO + 1, sl] * lo)
        comp0 = jnp.where(first, tnorm, aug)
        comp1 = jnp.where(first, pltpu.roll(tnorm, ATT_HEAD_DIM, 1), aug)
        o_ref[:, 2 * c * LANES:(2 * c + 1) * LANES] = comp0.astype(BF16)
        o_ref[:, (2 * c + 1) * LANES:(2 * c + 2) * LANES] = comp1.astype(BF16)


def _qk_table(g_q, g_k):
    n_groups = ATT_QK_WIDTH // ATT_HEAD_DIM
    slopes = 2.0 ** (-8.0 * jnp.arange(1, ATT_HEADS + 1, dtype=F32) / ATT_HEADS)
    slope_cols = jnp.repeat(slopes, 2 * ATT_HEAD_DIM)
    lane = jnp.tile(jnp.arange(LANES), ATT_HEADS) - ATT_HEAD_DIM
    at = lambda n: (lane == n).astype(F32)
    zeros = jnp.zeros(ATT_QK_WIDTH, F32)
    both = lambda q_part, k_part: jnp.concatenate([q_part, k_part])
    rows = [None] * TABLE_ROWS
    rows[ROW_GAIN] = both(jnp.tile(g_q.astype(F32) * ATT_HEAD_DIM ** -0.5, n_groups),
                          jnp.tile(g_k.astype(F32), n_groups))
    rows[ROW_ONES] = both(at(0) + at(1) + at(2), at(3) + at(4) + at(5))
    rows[ROW_SLOPE] = both(-slope_cols, slope_cols)
    rows[ROW_HI] = both(at(3), at(0))
    rows[ROW_MID] = both(at(4), at(1))
    rows[ROW_LO] = both(at(5), at(2))
    return jnp.stack([both(zeros, zeros) if r is None else r for r in rows])


def _qk_call(x, g_mix, w_in, table, *, tm, tn):
    s = x.shape[0]
    n_cols = 2 * ATT_QK_WIDTH
    return pl.pallas_call(
        functools.partial(_qk_kernel, tm=tm, tn=tn),
        grid=(s // tm, n_cols // tn),
        in_specs=[
            pl.BlockSpec((tm, D_MODEL), lambda i, j: (i, 0)),
            pl.BlockSpec((1, D_MODEL), lambda i, j: (0, 0)),
            pl.BlockSpec((D_MODEL, tn), lambda i, j: (0, j)),
            pl.BlockSpec((TABLE_ROWS, tn), lambda i, j: (0, j)),
        ],
        out_specs=pl.BlockSpec((tm, 2 * tn), lambda i, j: (i, j)),
        out_shape=jax.ShapeDtypeStruct((s, 2 * n_cols), BF16),
        scratch_shapes=[pltpu.VMEM((tm, D_MODEL), BF16)],
        compiler_params=_params("arbitrary", "arbitrary"),
        name="qk_proj",
    )(x, g_mix, w_in, table)


V_GROUP = 2 * LANES


def _v_kernel(x_ref, g_ref, w_ref, o_ref, h_ref, *, tm, tn):
    @pl.when(pl.program_id(1) == 0)
    def _():
        h_ref[...] = _rms_rows(x_ref[...], g_ref[...]).astype(BF16)

    acc = jnp.dot(h_ref[...], w_ref[...], preferred_element_type=F32).astype(BF16)
    ones_lane = (lax.broadcasted_iota(jnp.int32, (tm, LANES), 1) == 0).astype(BF16)
    for a in range(tn // ATT_V_DIM):
        o_ref[:, a * V_GROUP:a * V_GROUP + ATT_V_DIM] = acc[:, a * ATT_V_DIM:(a + 1) * ATT_V_DIM]
        o_ref[:, a * V_GROUP + ATT_V_DIM:(a + 1) * V_GROUP] = ones_lane


def _v_call(x, g_mix, w_in, *, tm, tn):
    s = x.shape[0]
    off = OFF_V // tn
    widen = V_GROUP // ATT_V_DIM
    return pl.pallas_call(
        functools.partial(_v_kernel, tm=tm, tn=tn),
        grid=(s // tm, ATT_WIDTH // tn),
        in_specs=[
            pl.BlockSpec((tm, D_MODEL), lambda i, j: (i, 0)),
            pl.BlockSpec((1, D_MODEL), lambda i, j: (0, 0)),
            pl.BlockSpec((D_MODEL, tn), lambda i, j: (0, off + j)),
        ],
        out_specs=pl.BlockSpec((tm, widen * tn), lambda i, j: (i, j)),
        out_shape=jax.ShapeDtypeStruct((s, widen * ATT_WIDTH), BF16),
        scratch_shapes=[pltpu.VMEM((tm, D_MODEL), BF16)],
        compiler_params=_params("arbitrary", "arbitrary"),
        name="v_proj",
    )(x, g_mix, w_in)


def _conv_kernel(x_ref, g_ref, wb_ref, wc_ref, wx_ref, cw_ref, o_ref, h_ref, carry_ref, *, tm):
    i = pl.program_id(0)
    j = pl.program_id(1)

    @pl.when(j == 0)
    def _():
        h_ref[...] = _rms_rows(x_ref[...], g_ref[...]).astype(BF16)

    @pl.when(i == 0)
    def _():
        carry_ref[j] = jnp.zeros(carry_ref.shape[1:], F32)

    h = h_ref[...]
    c_b = jnp.dot(h, wb_ref[...], preferred_element_type=F32)
    z = (jnp.dot(h, wc_ref[...], preferred_element_type=F32)
         * jnp.dot(h, wx_ref[...], preferred_element_type=F32))
    y = _causal_conv3(z, cw_ref[...], carry_ref[j])
    o_ref[...] = (c_b * y).astype(BF16)
    carry_ref[j] = z[tm - CARRY_ROWS:tm]


def _conv_call(x, g_mix, w_in, w_conv, *, tm, tn):
    s = x.shape[0]
    n_j = CONV_WIDTH // tn
    return pl.pallas_call(
        functools.partial(_conv_kernel, tm=tm),
        grid=(s // tm, n_j),
        in_specs=[
            pl.BlockSpec((tm, D_MODEL), lambda i, j: (i, 0)),
            pl.BlockSpec((1, D_MODEL), lambda i, j: (0, 0)),
            pl.BlockSpec((D_MODEL, tn), lambda i, j: (0, OFF_CB // tn + j)),
            pl.BlockSpec((D_MODEL, tn), lambda i, j: (0, OFF_CC // tn + j)),
            pl.BlockSpec((D_MODEL, tn), lambda i, j: (0, OFF_CX // tn + j)),
            pl.BlockSpec((3, tn), lambda i, j: (0, j)),
        ],
        out_specs=pl.BlockSpec((tm, tn), lambda i, j: (i, j)),
        out_shape=jax.ShapeDtypeStruct((s, CONV_WIDTH), BF16),
        scratch_shapes=[pltpu.VMEM((tm, D_MODEL), BF16),
                        pltpu.VMEM((n_j, CARRY_ROWS, tn), F32)],
        compiler_params=_params("arbitrary", "arbitrary"),
        name="conv_branch",
    )(x, g_mix, w_in, w_in, w_in, w_conv)


def _attn_kernel(lam_ref, gsub_ref, q0_ref, q1_ref, k0_ref, k1_ref, v_ref, o_ref,
                 m_ref, l_ref, acc_ref, *, t, lambda_init):
    i = pl.program_id(1)
    m_ref[...] = jnp.full(m_ref.shape, MASK_VALUE, F32)
    l_ref[...] = jnp.zeros(l_ref.shape, F32)
    acc_ref[...] = jnp.zeros(acc_ref.shape, F32)

    q = (q0_ref[...], q1_ref[...])
    k_refs = (k0_ref, k1_ref)
    n_chunks = t // LANES

    def tile(j, masked):
        start = pl.multiple_of(j * t, t)
        v = v_ref[pl.ds(start, t), :]
        if masked:
            keep = (lax.broadcasted_iota(jnp.int32, (t, t), 1)
                    <= lax.broadcasted_iota(jnp.int32, (t, t), 0))
        for c in range(2):
            k = k_refs[c][pl.ds(start, t), :]
            s = lax.dot_general(q[c], k, NT_DIMS, preferred_element_type=F32)
            if masked:
                s = jnp.where(keep, s, MASK_VALUE)
            chunks = [s[:, a * LANES:(a + 1) * LANES] for a in range(n_chunks)]
            m_prev = m_ref[c]
            m_new = jnp.maximum(m_prev, jnp.max(functools.reduce(jnp.maximum, chunks),
                                                axis=1, keepdims=True))
            alpha = jnp.exp(m_prev - m_new)
            ps = [jnp.exp(ch - m_new) for ch in chunks]
            l_ref[c] = alpha * l_ref[c] + functools.reduce(jnp.add, ps)
            p = jnp.concatenate([x.astype(BF16) for x in ps], axis=1)
            acc_ref[c] = alpha * acc_ref[c] + jnp.dot(p, v, preferred_element_type=F32)
            m_ref[c] = m_new

    def body(j, carry):
        tile(j, False)
        return carry

    lax.fori_loop(0, i, body, 0)
    tile(i, True)

    l0 = jnp.sum(l_ref[0], axis=1, keepdims=True)
    l1 = jnp.sum(l_ref[1], axis=1, keepdims=True)
    _attn_finish(acc_ref[0], l0, acc_ref[1], l1, lam_ref, gsub_ref, o_ref, lambda_init)


def _attn_finish(acc0, l0, acc1, l1, lam_ref, gsub_ref, o_ref, lambda_init):
    lam_rows = lam_ref[...]
    lam = (jnp.exp(jnp.sum(lam_rows[0:1] * lam_rows[1:2], axis=-1, keepdims=True))
           - jnp.exp(jnp.sum(lam_rows[2:3] * lam_rows[3:4], axis=-1, keepdims=True))
           + lambda_init)
    o = acc0 / l0 - lam * (acc1 / l1)
    o_ref[...] = (_rms_rows(o, gsub_ref[...]) * (1.0 - lambda_init)).astype(BF16)


def _attn_bounded_kernel(lam_ref, gsub_ref, q0_ref, q1_ref, k0_ref, k1_ref, v_ref, o_ref, acc_ref,
                         *, t, lambda_init, wide):
    i = pl.program_id(1)
    acc_ref[...] = jnp.zeros(acc_ref.shape, F32)
    q = (q0_ref[...], q1_ref[...])
    k_refs = (k0_ref, k1_ref)

    def tile(first, n_keys, masked):
        start = pl.multiple_of(first * t, t)
        v = v_ref[pl.ds(start, n_keys), :]
        if masked:
            keep = (lax.broadcasted_iota(jnp.int32, (t, n_keys), 1)
                    <= lax.broadcasted_iota(jnp.int32, (t, n_keys), 0))
        for c in range(2):
            k = k_refs[c][pl.ds(start, n_keys), :]
            s = lax.dot_general(q[c], k, NT_DIMS, preferred_element_type=F32)
            if masked:
                s = jnp.where(keep, s, MASK_VALUE)
            acc_ref[c] += jnp.dot(jnp.exp(s).astype(BF16), v, preferred_element_type=F32)

    n_wide = i // wide

    def wide_body(j, carry):
        tile(j * wide, wide * t, False)
        return carry

    def single_body(j, carry):
        tile(j, t, False)
        return carry

    lax.fori_loop(0, n_wide, wide_body, 0)
    lax.fori_loop(n_wide * wide, i, single_body, 0)
    tile(i, t, True)

    acc0 = acc_ref[0]
    acc1 = acc_ref[1]
    _attn_finish(acc0[:, :ATT_V_DIM], acc0[:, ATT_V_DIM:ATT_V_DIM + 1],
                 acc1[:, :ATT_V_DIM], acc1[:, ATT_V_DIM:ATT_V_DIM + 1],
                 lam_ref, gsub_ref, o_ref, lambda_init)


ATT_WIDE_TILES = 4


def _attn_call(lam_rows, g_sub, qk, v, *, t, lambda_init, bounded):
    s = v.shape[0]
    k_off = 2 * ATT_HEADS
    if bounded:
        body = functools.partial(_attn_bounded_kernel, wide=ATT_WIDE_TILES)
        v_spec = pl.BlockSpec((s, V_GROUP), lambda h, i: (0, h))
        scratch = [pltpu.VMEM((2, t, V_GROUP), F32)]
    else:
        body = _attn_kernel
        v_spec = pl.BlockSpec((s, ATT_V_DIM), lambda h, i: (0, (V_GROUP // ATT_V_DIM) * h))
        scratch = [pltpu.VMEM((2, t, LANES), F32),
                   pltpu.VMEM((2, t, LANES), F32),
                   pltpu.VMEM((2, t, ATT_V_DIM), F32)]
    return pl.pallas_call(
        functools.partial(body, t=t, lambda_init=lambda_init),
        grid=(ATT_HEADS, s // t),
        in_specs=[
            pl.BlockSpec((4, ATT_HEAD_DIM), lambda h, i: (0, 0)),
            pl.BlockSpec((1, ATT_V_DIM), lambda h, i: (0, 0)),
            pl.BlockSpec((t, LANES), lambda h, i: (i, 2 * h)),
            pl.BlockSpec((t, LANES), lambda h, i: (i, 2 * h + 1)),
            pl.BlockSpec((s, LANES), lambda h, i: (0, k_off + 2 * h)),
            pl.BlockSpec((s, LANES), lambda h, i: (0, k_off + 2 * h + 1)),
            v_spec,
        ],
        out_specs=pl.BlockSpec((t, ATT_V_DIM), lambda h, i: (i, h)),
        out_shape=jax.ShapeDtypeStruct((s, ATT_WIDTH), BF16),
        scratch_shapes=scratch,
        compiler_params=_params("arbitrary", "arbitrary"),
        name="diff_attn_bounded" if bounded else "diff_attn",
    )(lam_rows, g_sub, qk, qk, qk, qk, v)


SCORE_BOUND_LIMIT = 40.0


def _attention(lam_rows, g_sub, g_q, g_k, qk, v, *, t, lambda_init):
    bound = (1.01 * ATT_HEAD_DIM ** 0.5
             * jnp.max(jnp.abs(g_q.astype(F32))) * jnp.max(jnp.abs(g_k.astype(F32))))
    call = functools.partial(_attn_call, t=t, lambda_init=lambda_init)
    return lax.cond(bound <= SCORE_BOUND_LIMIT,
                    functools.partial(call, bounded=True),
                    functools.partial(call, bounded=False),
                    lam_rows, g_sub, qk, v)


def _merge_kernel(x_ref, g_ref, ya_ref, yc_ref, wga_ref, wgc_ref, bg_ref, wa_ref, wc_ref, wo_ref,
                  gffn_ref, x1_ref, h2_ref):
    xf = x_ref[...]
    h = _rms_rows(xf, g_ref[...]).astype(BF16)
    bg = bg_ref[...]
    gate_a = jax.nn.sigmoid(jnp.dot(h, wga_ref[...], preferred_element_type=F32) + bg[0:1])
    merged = gate_a * jnp.dot(ya_ref[...], wa_ref[...], preferred_element_type=F32)
    gate_c = jax.nn.sigmoid(jnp.dot(h, wgc_ref[...], preferred_element_type=F32) + bg[1:2])
    merged = merged + gate_c * jnp.dot(yc_ref[...], wc_ref[...], preferred_element_type=F32)
    x1 = xf + jnp.dot(merged.astype(BF16), wo_ref[...], preferred_element_type=F32)
    x1_ref[...] = x1
    h2_ref[...] = _rms_rows(x1, gffn_ref[...]).astype(BF16)


def _merge_call(x, g_mix, y_att, y_conv, w_in, b_gates, w_att_out, w_conv_out, w_o, g_ffn, *, tm):
    s = x.shape[0]
    const = lambda i: (0, 0)
    return pl.pallas_call(
        _merge_kernel,
        grid=(s // tm,),
        in_specs=[
            pl.BlockSpec((tm, D_MODEL), lambda i: (i, 0)),
            _resident((1, D_MODEL), const),
            pl.BlockSpec((tm, ATT_WIDTH), lambda i: (i, 0)),
            pl.BlockSpec((tm, CONV_WIDTH), lambda i: (i, 0)),
            _resident((D_MODEL, D_MODEL), lambda i: (0, OFF_GA // D_MODEL)),
            _resident((D_MODEL, D_MODEL), lambda i: (0, OFF_GC // D_MODEL)),
            _resident((2, D_MODEL), const),
            _resident((ATT_WIDTH, D_MODEL), const),
            _resident((CONV_WIDTH, D_MODEL), const),
            _resident((D_MODEL, D_MODEL), const),
            _resident((1, D_MODEL), const),
        ],
        out_specs=[pl.BlockSpec((tm, D_MODEL), lambda i: (i, 0)),
                   pl.BlockSpec((tm, D_MODEL), lambda i: (i, 0))],
        out_shape=[jax.ShapeDtypeStruct((s, D_MODEL), F32),
                   jax.ShapeDtypeStruct((s, D_MODEL), BF16)],
        compiler_params=_params("arbitrary"),
        name="merge",
    )(x, g_mix, y_att, y_conv, w_in, w_in, b_gates, w_att_out, w_conv_out, w_o, g_ffn)


FFN_SLICE = 256


def _ffn_kernel(h_ref, x1_ref, wg_ref, wv_ref, cwg_ref, cwv_ref, wd_ref, o_ref,
                carry_ref, pre_ref, act_ref, *, tm):
    i = pl.program_id(0)
    f = pl.program_id(1)

    @pl.when(i == 0)
    def _():
        carry_ref[f] = jnp.zeros(carry_ref.shape[1:], F32)

    @pl.when(f == 0)
    def _():
        o_ref[...] = x1_ref[...]

    h = h_ref[...]
    n_slices = wg_ref.shape[1] // FFN_SLICE
    slices = [slice(a * FFN_SLICE, (a + 1) * FFN_SLICE) for a in range(n_slices)]
    for a, sl in enumerate(slices):
        for b, (w_ref, cw_ref) in enumerate(((wg_ref, cwg_ref), (wv_ref, cwv_ref))):
            pre_ref[b, a, 0:CARRY_ROWS] = carry_ref[f, b, :, sl]
            pre_ref[b, a, CARRY_ROWS:CARRY_ROWS + tm] = jnp.dot(h, w_ref[:, sl],
                                                                preferred_element_type=F32)
            carry_ref[f, b, :, sl] = pre_ref[b, a, tm:tm + CARRY_ROWS]
    for a, sl in enumerate(slices):
        u = []
        for b, cw_ref in enumerate((cwg_ref, cwv_ref)):
            w = cw_ref[:, sl]
            u.append(w[0:1] * pre_ref[b, a, CARRY_ROWS - 2:CARRY_ROWS - 2 + tm]
                     + w[1:2] * pre_ref[b, a, CARRY_ROWS - 1:CARRY_ROWS - 1 + tm]
                     + w[2:3] * pre_ref[b, a, CARRY_ROWS:CARRY_ROWS + tm])
        act_ref[a] = (u[0] * jax.nn.sigmoid(u[0]) * u[1]).astype(BF16)
    for a, sl in enumerate(slices):
        for n in range(o_ref.shape[1] // FFN_SLICE):
            cs = slice(n * FFN_SLICE, (n + 1) * FFN_SLICE)
            o_ref[:, cs] += jnp.dot(act_ref[a], wd_ref[sl, cs], preferred_element_type=F32)


def _ffn_call(h2, x1, w_up, w_ffn_conv, w_down, *, tm, tf):
    s = h2.shape[0]
    n_f = D_FF // tf
    return pl.pallas_call(
        functools.partial(_ffn_kernel, tm=tm),
        grid=(s // tm, n_f),
        in_specs=[
            pl.BlockSpec((tm, D_MODEL), lambda i, f: (i, 0)),
            pl.BlockSpec((tm, D_MODEL), lambda i, f: (i, 0)),
            pl.BlockSpec((D_MODEL, tf), lambda i, f: (0, f)),
            pl.BlockSpec((D_MODEL, tf), lambda i, f: (0, n_f + f)),
            pl.BlockSpec((3, tf), lambda i, f: (0, f)),
            pl.BlockSpec((3, tf), lambda i, f: (0, n_f + f)),
            pl.BlockSpec((tf, D_MODEL), lambda i, f: (f, 0)),
        ],
        out_specs=pl.BlockSpec((tm, D_MODEL), lambda i, f: (i, 0)),
        out_shape=jax.ShapeDtypeStruct((s, D_MODEL), F32),
        scratch_shapes=[pltpu.VMEM((n_f, 2, CARRY_ROWS, tf), F32),
                        pltpu.VMEM((2, tf // FFN_SLICE, CARRY_ROWS + tm, FFN_SLICE), F32),
                        pltpu.VMEM((tf // FFN_SLICE, tm, FFN_SLICE), BF16)],
        compiler_params=_params("arbitrary", "arbitrary"),
        name="conv_ffn",
    )(h2, x1, w_up, w_up, w_ffn_conv, w_ffn_conv, w_down)


def _ple_kernel(x_ref, p_ref, wple_ref, gple_ref, gpg_ref, wpg_ref, bpg_ref, o_ref):
    xf = x_ref[...]
    pe = _rms_rows(jnp.dot(p_ref[...].astype(BF16), wple_ref[...], preferred_element_type=F32),
                   gple_ref[...])
    hn = _rms_rows(xf, gpg_ref[...]).astype(BF16)
    pg = jax.nn.sigmoid(jnp.dot(hn, wpg_ref[...], preferred_element_type=F32) + bpg_ref[...])
    o_ref[...] = xf + pg * pe


def _ple_call(x2, p, w_ple, g_ple, g_pg, w_pg, b_pg, *, tm):
    s = x2.shape[0]
    const = lambda i: (0, 0)
    return pl.pallas_call(
        _ple_kernel,
        grid=(s // tm,),
        in_specs=[
            pl.BlockSpec((tm, D_MODEL), lambda i: (i, 0)),
            pl.BlockSpec((tm, PLE_DIM), lambda i: (i, 0)),
            _resident((PLE_DIM, D_MODEL), const),
            _resident((1, D_MODEL), const),
            _resident((1, D_MODEL), const),
            _resident((D_MODEL, D_MODEL), const),
            _resident((1, D_MODEL), const),
        ],
        out_specs=pl.BlockSpec((tm, D_MODEL), lambda i: (i, 0)),
        out_shape=jax.ShapeDtypeStruct((s, D_MODEL), F32),
        compiler_params=_params("arbitrary"),
        name="ple_gate",
    )(x2, p, w_ple, g_ple, g_pg, w_pg, b_pg)


def _layer(depth_index, x, p, g_mix, w_in, b_gates, g_q, g_k, lam_rows, g_sub, w_conv_mix, w_att_out,
           w_conv_out, w_o, g_ffn, w_up, w_ffn_conv, w_down, w_ple, g_ple, g_pg, w_pg, b_pg):
    lambda_init = 0.8 - 0.6 * math.exp(-0.3 * depth_index)
    row = lambda a: a.reshape(1, -1).astype(F32)

    w_in = w_in.astype(BF16)
    g_mix = row(g_mix)
    qk = _qk_call(x, g_mix, w_in, _qk_table(g_q, g_k), tm=512, tn=256)
    v = _v_call(x, g_mix, w_in, tm=512, tn=512)
    y_conv = _conv_call(x, g_mix, w_in, w_conv_mix.astype(F32), tm=512, tn=256)
    y_att = _attention(lam_rows, row(g_sub), g_q, g_k, qk, v, t=512, lambda_init=lambda_init)
    x1, h2 = _merge_call(x, g_mix, y_att, y_conv, w_in, b_gates.astype(F32),
                         w_att_out.astype(BF16), w_conv_out.astype(BF16), w_o.astype(BF16),
                         row(g_ffn), tm=256)
    x2 = _ffn_call(h2, x1, w_up.astype(BF16), w_ffn_conv.astype(F32), w_down.astype(BF16),
                   tm=512, tf=512)
    return _ple_call(x2, p, w_ple.astype(BF16), row(g_ple), row(g_pg), w_pg.astype(BF16),
                     row(b_pg), tm=256)


def kernel(x, p, g_mix, w_in, b_gates, g_q, g_k, lam_q1, lam_k1, lam_q2, lam_k2, g_sub, w_conv_mix,
           w_att_out, w_conv_out, w_o, g_ffn, w_up, w_ffn_conv, w_down, w_ple, g_ple, g_pg, w_pg,
           b_pg):
    batch, seq, d_model = x.shape
    depth = p.shape[0]
    outs = []
    for b in range(batch):
        xb = x.reshape(seq, d_model) if batch == 1 else x[b]
        for i in range(depth):
            lam_rows = jnp.stack([lam_q1[i], lam_k1[i], lam_q2[i], lam_k2[i]]).astype(F32)
            pb = p.reshape(seq, PLE_DIM) if batch == 1 and depth == 1 else p[i, b]
            xb = _layer(i, xb, pb, g_mix[i], w_in[i], b_gates[i], g_q[i], g_k[i], lam_rows,
                        g_sub[i], w_conv_mix[i], w_att_out[i], w_conv_out[i], w_o[i], g_ffn[i],
                        w_up[i], w_ffn_conv[i], w_down[i], w_ple[i], g_ple[i], g_pg[i], w_pg[i],
                        b_pg[i])
        outs.append(xb)
    return outs[0].reshape(x.shape) if batch == 1 else jnp.stack(outs)
```

```python
import functools
import math

import jax
import jax.numpy as jnp
import numpy as np
from jax import lax
from jax.experimental import pallas as pl
from jax.experimental.pallas import tpu as pltpu

F32 = jnp.float32
BF16 = jnp.bfloat16

D_MODEL = 2048
ATT_HEADS = 8
ATT_HEAD_DIM = 64
ATT_V_DIM = 2 * ATT_HEAD_DIM
ATT_QK_WIDTH = ATT_HEADS * 2 * ATT_HEAD_DIM
ATT_WIDTH = ATT_HEADS * ATT_V_DIM
CONV_WIDTH = 1024
D_FF = 5632
PLE_DIM = 256
EPS = 1e-6

LANES = 128
CARRY_ROWS = 8
MASK_VALUE = -1e30
VMEM_LIMIT = 56 * 1024 * 1024

OFF_Q = 0
OFF_K = ATT_QK_WIDTH
OFF_V = 2 * ATT_QK_WIDTH
OFF_CB = OFF_V + ATT_WIDTH
OFF_CC = OFF_CB + CONV_WIDTH
OFF_CX = OFF_CC + CONV_WIDTH
OFF_GA = OFF_CX + CONV_WIDTH
OFF_GC = OFF_GA + D_MODEL

NT_DIMS = (((1,), (1,)), ((), ()))


def _rms_rows(xf, g):
    ms = jnp.mean(xf * xf, axis=-1, keepdims=True)
    return xf * lax.rsqrt(ms + EPS) * g


def _params(*sem):
    return pltpu.CompilerParams(dimension_semantics=sem, vmem_limit_bytes=VMEM_LIMIT)


def _resident(shape, index_map):
    return pl.BlockSpec(shape, index_map, pipeline_mode=pl.Buffered(1))


def _shift_rows(u, prev, rows):
    u1 = jnp.where(rows == 0, prev[CARRY_ROWS - 1:CARRY_ROWS], pltpu.roll(u, 1, 0))
    u2 = jnp.where(rows == 0, prev[CARRY_ROWS - 2:CARRY_ROWS - 1],
                   jnp.where(rows == 1, prev[CARRY_ROWS - 1:CARRY_ROWS], pltpu.roll(u, 2, 0)))
    return u1, u2


def _causal_conv3(u, w, prev):
    rows = lax.broadcasted_iota(jnp.int32, u.shape, 0)
    u1, u2 = _shift_rows(u, prev, rows)
    return w[0:1] * u2 + w[1:2] * u1 + w[2:3] * u


ROW_GAIN, ROW_ONES, ROW_SLOPE, ROW_HI, ROW_MID, ROW_LO, TABLE_ROWS = 0, 1, 2, 3, 4, 5, 8
V_GROUP = 2 * LANES
PROJ_TILE = 256
N_QK_TILES = 2 * ATT_QK_WIDTH // PROJ_TILE
N_QKV_TILES = N_QK_TILES + ATT_WIDTH // PROJ_TILE
N_CONV_TILES = CONV_WIDTH // PROJ_TILE
QKV_OUT_TILE = 2 * PROJ_TILE
QKV_OUT_WIDTH = N_QKV_TILES * QKV_OUT_TILE


def _qk_tile(acc, tab_ref, o_ref, first_pos):
    tm = acc.shape[0]
    lane = lax.broadcasted_iota(jnp.int32, (tm, LANES), 1)
    pos = (first_pos + lax.broadcasted_iota(jnp.int32, (tm, LANES), 0)).astype(F32)
    first = lane < ATT_HEAD_DIM
    for c in range(PROJ_TILE // LANES):
        sl = slice(c * LANES, (c + 1) * LANES)
        t = acc[:, sl]
        sq = t * t
        ss0 = jnp.sum(jnp.where(first, sq, 0.0), axis=-1, keepdims=True)
        ss1 = jnp.sum(jnp.where(first, 0.0, sq), axis=-1, keepdims=True)
        r = jnp.where(first, lax.rsqrt(ss0 / ATT_HEAD_DIM + EPS), lax.rsqrt(ss1 / ATT_HEAD_DIM + EPS))
        tnorm = t * r * tab_ref[ROW_GAIN:ROW_GAIN + 1, sl]
        bias = tab_ref[ROW_SLOPE:ROW_SLOPE + 1, sl] * pos
        hi = bias.astype(BF16).astype(F32)
        rem = bias - hi
        mid = rem.astype(BF16).astype(F32)
        lo = rem - mid
        aug = (tab_ref[ROW_ONES:ROW_ONES + 1, sl] + tab_ref[ROW_HI:ROW_HI + 1, sl] * hi
               + tab_ref[ROW_MID:ROW_MID + 1, sl] * mid + tab_ref[ROW_LO:ROW_LO + 1, sl] * lo)
        comp0 = jnp.where(first, tnorm, aug)
        comp1 = jnp.where(first, pltpu.roll(tnorm, ATT_HEAD_DIM, 1), aug)
        o_ref[:, 2 * c * LANES:(2 * c + 1) * LANES] = comp0.astype(BF16)
        o_ref[:, (2 * c + 1) * LANES:(2 * c + 2) * LANES] = comp1.astype(BF16)


def _v_tile(acc, o_ref):
    tm = acc.shape[0]
    ones_lane = (lax.broadcasted_iota(jnp.int32, (tm, LANES), 1) == 0).astype(BF16)
    for a in range(PROJ_TILE // ATT_V_DIM):
        o_ref[:, a * V_GROUP:a * V_GROUP + ATT_V_DIM] = acc[:, a * ATT_V_DIM:(a + 1) * ATT_V_DIM].astype(BF16)
        o_ref[:, a * V_GROUP + ATT_V_DIM:(a + 1) * V_GROUP] = ones_lane


def _inproj_kernel(x_ref, g_ref, w_ref, wb_ref, wc_ref, wx_ref, tab_ref, cw_ref, qkv_ref, conv_ref,
                   h_ref, carry_ref, *, tm):
    i = pl.program_id(0)
    j = pl.program_id(1)

    @pl.when(j == 0)
    def _():
        h_ref[...] = _rms_rows(x_ref[...], g_ref[...]).astype(BF16)

    @pl.when(j < N_QK_TILES)
    def _():
        acc = jnp.dot(h_ref[...], w_ref[...], preferred_element_type=F32)
        _qk_tile(acc, tab_ref, qkv_ref, i * tm)

    @pl.when(jnp.logical_and(j >= N_QK_TILES, j < N_QKV_TILES))
    def _():
        _v_tile(jnp.dot(h_ref[...], w_ref[...], preferred_element_type=F32), qkv_ref)

    @pl.when(j >= N_QKV_TILES)
    def _():
        jc = j - N_QKV_TILES

        @pl.when(i == 0)
        def _():
            carry_ref[jc] = jnp.zeros(carry_ref.shape[1:], F32)

        h = h_ref[...]
        c_b = jnp.dot(h, wb_ref[...], preferred_element_type=F32)
        z = (jnp.dot(h, wc_ref[...], preferred_element_type=F32)
             * jnp.dot(h, wx_ref[...], preferred_element_type=F32))
        y = _causal_conv3(z, cw_ref[...], carry_ref[jc])
        conv_ref[...] = (c_b * y).astype(BF16)
        carry_ref[jc] = z[tm - CARRY_ROWS:tm]


def _qk_table(g_q, g_k):
    n_groups = ATT_QK_WIDTH // ATT_HEAD_DIM
    slopes = jnp.asarray(2.0 ** (-8.0 * np.arange(1, ATT_HEADS + 1) / ATT_HEADS), F32)
    slope_cols = jnp.repeat(slopes, 2 * ATT_HEAD_DIM)
    lane = jnp.tile(jnp.arange(LANES), ATT_HEADS) - ATT_HEAD_DIM
    at = lambda n: (lane == n).astype(F32)
    zeros = jnp.zeros(ATT_QK_WIDTH, F32)
    both = lambda q_part, k_part: jnp.concatenate([q_part, k_part])
    rows = [None] * TABLE_ROWS
    rows[ROW_GAIN] = both(jnp.tile(g_q.astype(F32) * ATT_HEAD_DIM ** -0.5, n_groups),
                          jnp.tile(g_k.astype(F32), n_groups))
    rows[ROW_ONES] = both(at(0) + at(1) + at(2), at(3) + at(4) + at(5))
    rows[ROW_SLOPE] = both(-slope_cols, slope_cols)
    rows[ROW_HI] = both(at(3), at(0))
    rows[ROW_MID] = both(at(4), at(1))
    rows[ROW_LO] = both(at(5), at(2))
    return jnp.stack([both(zeros, zeros) if r is None else r for r in rows])


def _inproj_call(x, g_mix, w_in, table, w_conv, *, tm):
    s = x.shape[0]
    qkv_tile = lambda j: jnp.minimum(j, N_QKV_TILES - 1)
    conv_tile = lambda j: jnp.maximum(j - N_QKV_TILES, 0)
    conv_w = lambda off: pl.BlockSpec((D_MODEL, PROJ_TILE),
                                      lambda i, j: (0, off // PROJ_TILE + conv_tile(j)))
    return pl.pallas_call(
        functools.partial(_inproj_kernel, tm=tm),
        grid=(s // tm, N_QKV_TILES + N_CONV_TILES),
        in_specs=[
            pl.BlockSpec((tm, D_MODEL), lambda i, j: (i, 0)),
            pl.BlockSpec((1, D_MODEL), lambda i, j: (0, 0)),
            pl.BlockSpec((D_MODEL, PROJ_TILE), lambda i, j: (0, qkv_tile(j))),
            conv_w(OFF_CB), conv_w(OFF_CC), conv_w(OFF_CX),
            pl.BlockSpec((TABLE_ROWS, PROJ_TILE), lambda i, j: (0, jnp.minimum(j, N_QK_TILES - 1))),
            pl.BlockSpec((3, PROJ_TILE), lambda i, j: (0, conv_tile(j))),
        ],
        out_specs=[pl.BlockSpec((tm, QKV_OUT_TILE), lambda i, j: (i, qkv_tile(j))),
                   pl.BlockSpec((tm, PROJ_TILE), lambda i, j: (i, conv_tile(j)))],
        out_shape=[jax.ShapeDtypeStruct((s, QKV_OUT_WIDTH), BF16),
                   jax.ShapeDtypeStruct((s, CONV_WIDTH), BF16)],
        scratch_shapes=[pltpu.VMEM((tm, D_MODEL), BF16),
                        pltpu.VMEM((N_CONV_TILES, CARRY_ROWS, PROJ_TILE), F32)],
        compiler_params=_params("arbitrary", "arbitrary"),
        name="in_proj",
    )(x, g_mix, w_in, w_in, w_in, w_in, table, w_conv)


def _attn_kernel(lam_ref, gsub_ref, q0_ref, q1_ref, k0_ref, k1_ref, v_ref, o_ref,
                 m_ref, l_ref, acc_ref, *, t, lambda_init):
    i = pl.program_id(1)
    m_ref[...] = jnp.full(m_ref.shape, MASK_VALUE, F32)
    l_ref[...] = jnp.zeros(l_ref.shape, F32)
    acc_ref[...] = jnp.zeros(acc_ref.shape, F32)

    q = (q0_ref[...], q1_ref[...])
    k_refs = (k0_ref, k1_ref)
    n_chunks = t // LANES

    def tile(j, masked):
        start = pl.multiple_of(j * t, t)
        v = v_ref[pl.ds(start, t), :]
        if masked:
            keep = (lax.broadcasted_iota(jnp.int32, (t, t), 1)
                    <= lax.broadcasted_iota(jnp.int32, (t, t), 0))
        for c in range(2):
            k = k_refs[c][pl.ds(start, t), :]
            s = lax.dot_general(q[c], k, NT_DIMS, preferred_element_type=F32)
            if masked:
                s = jnp.where(keep, s, MASK_VALUE)
            chunks = [s[:, a * LANES:(a + 1) * LANES] for a in range(n_chunks)]
            m_prev = m_ref[c]
            m_new = jnp.maximum(m_prev, jnp.max(functools.reduce(jnp.maximum, chunks),
                                                axis=1, keepdims=True))
            alpha = jnp.exp(m_prev - m_new)
            ps = [jnp.exp(ch - m_new) for ch in chunks]
            l_ref[c] = alpha * l_ref[c] + functools.reduce(jnp.add, ps)
            p = jnp.concatenate([x.astype(BF16) for x in ps], axis=1)
            acc_ref[c] = alpha * acc_ref[c] + jnp.dot(p, v, preferred_element_type=F32)
            m_ref[c] = m_new

    def body(j, carry):
        tile(j, False)
        return carry

    lax.fori_loop(0, i, body, 0)
    tile(i, True)

    l0 = jnp.sum(l_ref[0], axis=1, keepdims=True)
    l1 = jnp.sum(l_ref[1], axis=1, keepdims=True)
    _attn_finish(acc_ref[0], l0, acc_ref[1], l1, lam_ref, gsub_ref, o_ref, lambda_init)


def _attn_finish(acc0, l0, acc1, l1, lam_ref, gsub_ref, o_ref, lambda_init):
    lam_rows = lam_ref[...]
    lam = (jnp.exp(jnp.sum(lam_rows[0:1] * lam_rows[1:2], axis=-1, keepdims=True))
           - jnp.exp(jnp.sum(lam_rows[2:3] * lam_rows[3:4], axis=-1, keepdims=True))
           + lambda_init)
    o = acc0 / l0 - lam * (acc1 / l1)
    o_ref[...] = (_rms_rows(o, gsub_ref[...]) * (1.0 - lambda_init)).astype(BF16)


def _attn_bounded_kernel(lam_ref, gsub_ref, q0_ref, q1_ref, k0_ref, k1_ref, v_ref, o_ref, acc_ref,
                         *, t, lambda_init, wide):
    i = pl.program_id(1)
    acc_ref[...] = jnp.zeros(acc_ref.shape, F32)
    q = (q0_ref[...], q1_ref[...])
    k_refs = (k0_ref, k1_ref)

    def tile(first, n_keys, masked):
        start = pl.multiple_of(first * t, t)
        v = v_ref[pl.ds(start, n_keys), :]
        if masked:
            keep = (lax.broadcasted_iota(jnp.int32, (t, n_keys), 1)
                    <= lax.broadcasted_iota(jnp.int32, (t, n_keys), 0))
        for c in range(2):
            k = k_refs[c][pl.ds(start, n_keys), :]
            s = lax.dot_general(q[c], k, NT_DIMS, preferred_element_type=F32)
            if masked:
                s = jnp.where(keep, s, MASK_VALUE)
            acc_ref[c] += jnp.dot(jnp.exp(s).astype(BF16), v, preferred_element_type=F32)

    n_wide = i // wide

    def wide_body(j, carry):
        tile(j * wide, wide * t, False)
        return carry

    def single_body(j, carry):
        tile(j, t, False)
        return carry

    lax.fori_loop(0, n_wide, wide_body, 0)
    lax.fori_loop(n_wide * wide, i, single_body, 0)
    tile(i, t, True)

    acc0 = acc_ref[0]
    acc1 = acc_ref[1]
    _attn_finish(acc0[:, :ATT_V_DIM], acc0[:, ATT_V_DIM:ATT_V_DIM + 1],
                 acc1[:, :ATT_V_DIM], acc1[:, ATT_V_DIM:ATT_V_DIM + 1],
                 lam_ref, gsub_ref, o_ref, lambda_init)


ATT_WIDE_TILES = 4


def _attn_call(lam_rows, g_sub, qkv, *, t, lambda_init, bounded):
    s = qkv.shape[0]
    k_off = 2 * ATT_HEADS
    v_off = 2 * ATT_QK_WIDTH // ATT_HEAD_DIM
    if bounded:
        body = functools.partial(_attn_bounded_kernel, wide=ATT_WIDE_TILES)
        v_spec = pl.BlockSpec((s, V_GROUP), lambda h, i: (0, v_off * LANES // V_GROUP + h))
        scratch = [pltpu.VMEM((2, t, V_GROUP), F32)]
    else:
        body = _attn_kernel
        v_spec = pl.BlockSpec((s, ATT_V_DIM), lambda h, i: (0, v_off + (V_GROUP // ATT_V_DIM) * h))
        scratch = [pltpu.VMEM((2, t, LANES), F32),
                   pltpu.VMEM((2, t, LANES), F32),
                   pltpu.VMEM((2, t, ATT_V_DIM), F32)]
    return pl.pallas_call(
        functools.partial(body, t=t, lambda_init=lambda_init),
        grid=(ATT_HEADS, s // t),
        in_specs=[
            pl.BlockSpec((4, ATT_HEAD_DIM), lambda h, i: (0, 0)),
            pl.BlockSpec((1, ATT_V_DIM), lambda h, i: (0, 0)),
            pl.BlockSpec((t, LANES), lambda h, i: (i, 2 * h)),
            pl.BlockSpec((t, LANES), lambda h, i: (i, 2 * h + 1)),
            pl.BlockSpec((s, LANES), lambda h, i: (0, k_off + 2 * h)),
            pl.BlockSpec((s, LANES), lambda h, i: (0, k_off + 2 * h + 1)),
            v_spec,
        ],
        out_specs=pl.BlockSpec((t, ATT_V_DIM), lambda h, i: (i, h)),
        out_shape=jax.ShapeDtypeStruct((s, ATT_WIDTH), BF16),
        scratch_shapes=scratch,
        compiler_params=_params("arbitrary", "arbitrary"),
        name="diff_attn_bounded" if bounded else "diff_attn",
    )(lam_rows, g_sub, qkv, qkv, qkv, qkv, qkv)


SCORE_BOUND_LIMIT = 40.0


def _attention(lam_rows, g_sub, g_q, g_k, qkv, *, t, lambda_init):
    bound = (1.01 * ATT_HEAD_DIM ** 0.5
             * jnp.max(jnp.abs(g_q.astype(F32))) * jnp.max(jnp.abs(g_k.astype(F32))))
    call = functools.partial(_attn_call, t=t, lambda_init=lambda_init)
    return lax.cond(bound <= SCORE_BOUND_LIMIT,
                    functools.partial(call, bounded=True),
                    functools.partial(call, bounded=False),
                    lam_rows, g_sub, qkv)


def _merge_kernel(x_ref, g_ref, ya_ref, yc_ref, wga_ref, wgc_ref, bg_ref, wa_ref, wc_ref, wo_ref,
                  gffn_ref, x1_ref, h2_ref):
    xf = x_ref[...]
    h = _rms_rows(xf, g_ref[...]).astype(BF16)
    bg = bg_ref[...]
    gate_a = jax.nn.sigmoid(jnp.dot(h, wga_ref[...], preferred_element_type=F32) + bg[0:1])
    merged = gate_a * jnp.dot(ya_ref[...], wa_ref[...], preferred_element_type=F32)
    gate_c = jax.nn.sigmoid(jnp.dot(h, wgc_ref[...], preferred_element_type=F32) + bg[1:2])
    merged = merged + gate_c * jnp.dot(yc_ref[...], wc_ref[...], preferred_element_type=F32)
    x1 = xf + jnp.dot(merged.astype(BF16), wo_ref[...], preferred_element_type=F32)
    x1_ref[...] = x1
    h2_ref[...] = _rms_rows(x1, gffn_ref[...]).astype(BF16)


def _merge_call(x, g_mix, y_att, y_conv, w_in, b_gates, w_att_out, w_conv_out, w_o, g_ffn, *, tm):
    s = x.shape[0]
    const = lambda i: (0, 0)
    return pl.pallas_call(
        _merge_kernel,
        grid=(s // tm,),
        in_specs=[
            pl.BlockSpec((tm, D_MODEL), lambda i: (i, 0)),
            _resident((1, D_MODEL), const),
            pl.BlockSpec((tm, ATT_WIDTH), lambda i: (i, 0)),
            pl.BlockSpec((tm, CONV_WIDTH), lambda i: (i, 0)),
            _resident((D_MODEL, D_MODEL), lambda i: (0, OFF_GA // D_MODEL)),
            _resident((D_MODEL, D_MODEL), lambda i: (0, OFF_GC // D_MODEL)),
            _resident((2, D_MODEL), const),
            _resident((ATT_WIDTH, D_MODEL), const),
            _resident((CONV_WIDTH, D_MODEL), const),
            _resident((D_MODEL, D_MODEL), const),
            _resident((1, D_MODEL), const),
        ],
        out_specs=[pl.BlockSpec((tm, D_MODEL), lambda i: (i, 0)),
                   pl.BlockSpec((tm, D_MODEL), lambda i: (i, 0))],
        out_shape=[jax.ShapeDtypeStruct((s, D_MODEL), F32),
                   jax.ShapeDtypeStruct((s, D_MODEL), BF16)],
        compiler_params=_params("arbitrary"),
        name="merge",
    )(x, g_mix, y_att, y_conv, w_in, w_in, b_gates, w_att_out, w_conv_out, w_o, g_ffn)


FFN_SLICE = 256


def _ffn_kernel(h_ref, wg_ref, wv_ref, cwg_ref, cwv_ref, wd_ref, o_ref,
                carry_ref, pre_ref, act_ref, *, tm):
    i = pl.program_id(0)
    f = pl.program_id(1)

    @pl.when(i == 0)
    def _():
        carry_ref[f] = jnp.zeros(carry_ref.shape[1:], F32)

    @pl.when(f == 0)
    def _():
        o_ref[...] = jnp.zeros(o_ref.shape, F32)

    h = h_ref[...]
    n_slices = wg_ref.shape[1] // FFN_SLICE
    slices = [slice(a * FFN_SLICE, (a + 1) * FFN_SLICE) for a in range(n_slices)]
    for a, sl in enumerate(slices):
        for b, (w_ref, cw_ref) in enumerate(((wg_ref, cwg_ref), (wv_ref, cwv_ref))):
            pre_ref[b, a, 0:CARRY_ROWS] = carry_ref[f, b, :, sl]
            pre_ref[b, a, CARRY_ROWS:CARRY_ROWS + tm] = jnp.dot(h, w_ref[:, sl],
                                                                preferred_element_type=F32)
            carry_ref[f, b, :, sl] = pre_ref[b, a, tm:tm + CARRY_ROWS]
    for a, sl in enumerate(slices):
        u = []
        for b, cw_ref in enumerate((cwg_ref, cwv_ref)):
            w = cw_ref[:, sl]
            u.append(w[0:1] * pre_ref[b, a, CARRY_ROWS - 2:CARRY_ROWS - 2 + tm]
                     + w[1:2] * pre_ref[b, a, CARRY_ROWS - 1:CARRY_ROWS - 1 + tm]
                     + w[2:3] * pre_ref[b, a, CARRY_ROWS:CARRY_ROWS + tm])
        act_ref[a] = (u[0] * jax.nn.sigmoid(u[0]) * u[1]).astype(BF16)
    for a, sl in enumerate(slices):
        for n in range(o_ref.shape[1] // FFN_SLICE):
            cs = slice(n * FFN_SLICE, (n + 1) * FFN_SLICE)
            o_ref[:, cs] += jnp.dot(act_ref[a], wd_ref[sl, cs], preferred_element_type=F32)


def _ffn_call(h2, w_up, w_ffn_conv, w_down, *, tm, tf):
    s = h2.shape[0]
    n_f = D_FF // tf
    return pl.pallas_call(
        functools.partial(_ffn_kernel, tm=tm),
        grid=(s // tm, n_f),
        in_specs=[
            pl.BlockSpec((tm, D_MODEL), lambda i, f: (i, 0)),
            pl.BlockSpec((D_MODEL, tf), lambda i, f: (0, f)),
            pl.BlockSpec((D_MODEL, tf), lambda i, f: (0, n_f + f)),
            pl.BlockSpec((3, tf), lambda i, f: (0, f)),
            pl.BlockSpec((3, tf), lambda i, f: (0, n_f + f)),
            pl.BlockSpec((tf, D_MODEL), lambda i, f: (f, 0)),
        ],
        out_specs=pl.BlockSpec((tm, D_MODEL), lambda i, f: (i, 0)),
        out_shape=jax.ShapeDtypeStruct((s, D_MODEL), F32),
        scratch_shapes=[pltpu.VMEM((n_f, 2, CARRY_ROWS, tf), F32),
                        pltpu.VMEM((2, tf // FFN_SLICE, CARRY_ROWS + tm, FFN_SLICE), F32),
                        pltpu.VMEM((tf // FFN_SLICE, tm, FFN_SLICE), BF16)],
        compiler_params=_params("arbitrary", "arbitrary"),
        name="conv_ffn",
    )(h2, w_up, w_up, w_ffn_conv, w_ffn_conv, w_down)


def _ple_kernel(x_ref, y_ref, p_ref, wple_ref, gple_ref, gpg_ref, wpg_ref, bpg_ref, o_ref):
    xf = x_ref[...] + y_ref[...]
    pe = _rms_rows(jnp.dot(p_ref[...].astype(BF16), wple_ref[...], preferred_element_type=F32),
                   gple_ref[...])
    hn = _rms_rows(xf, gpg_ref[...]).astype(BF16)
    pg = jax.nn.sigmoid(jnp.dot(hn, wpg_ref[...], preferred_element_type=F32) + bpg_ref[...])
    o_ref[...] = xf + pg * pe


def _ple_call(x1, y_ffn, p, w_ple, g_ple, g_pg, w_pg, b_pg, *, tm):
    s = x1.shape[0]
    const = lambda i: (0, 0)
    return pl.pallas_call(
        _ple_kernel,
        grid=(s // tm,),
        in_specs=[
            pl.BlockSpec((tm, D_MODEL), lambda i: (i, 0)),
            pl.BlockSpec((tm, D_MODEL), lambda i: (i, 0)),
            pl.BlockSpec((tm, PLE_DIM), lambda i: (i, 0)),
            _resident((PLE_DIM, D_MODEL), const),
            _resident((1, D_MODEL), const),
            _resident((1, D_MODEL), const),
            _resident((D_MODEL, D_MODEL), const),
            _resident((1, D_MODEL), const),
        ],
        out_specs=pl.BlockSpec((tm, D_MODEL), lambda i: (i, 0)),
        out_shape=jax.ShapeDtypeStruct((s, D_MODEL), F32),
        compiler_params=_params("arbitrary"),
        name="ple_gate",
    )(x1, y_ffn, p, w_ple, g_ple, g_pg, w_pg, b_pg)


def _layer(depth_index, x, p, g_mix, w_in, b_gates, g_q, g_k, lam_rows, g_sub, w_conv_mix, w_att_out,
           w_conv_out, w_o, g_ffn, w_up, w_ffn_conv, w_down, w_ple, g_ple, g_pg, w_pg, b_pg):
    lambda_init = 0.8 - 0.6 * math.exp(-0.3 * depth_index)
    row = lambda a: a.reshape(1, -1).astype(F32)

    w_in = w_in.astype(BF16)
    g_mix = row(g_mix)
    qkv, y_conv = _inproj_call(x, g_mix, w_in, _qk_table(g_q, g_k), w_conv_mix.astype(F32), tm=512)
    y_att = _attention(lam_rows, row(g_sub), g_q, g_k, qkv, t=512, lambda_init=lambda_init)
    x1, h2 = _merge_call(x, g_mix, y_att, y_conv, w_in, b_gates.astype(F32),
                         w_att_out.astype(BF16), w_conv_out.astype(BF16), w_o.astype(BF16),
                         row(g_ffn), tm=256)
    y_ffn = _ffn_call(h2, w_up.astype(BF16), w_ffn_conv.astype(F32), w_down.astype(BF16),
                      tm=1024, tf=512)
    return _ple_call(x1, y_ffn, p, w_ple.astype(BF16), row(g_ple), row(g_pg), w_pg.astype(BF16),
                     row(b_pg), tm=256)


def kernel(x, p, g_mix, w_in, b_gates, g_q, g_k, lam_q1, lam_k1, lam_q2, lam_k2, g_sub, w_conv_mix,
           w_att_out, w_conv_out, w_o, g_ffn, w_up, w_ffn_conv, w_down, w_ple, g_ple, g_pg, w_pg,
           b_pg):
    batch, seq, d_model = x.shape
    depth = p.shape[0]
    outs = []
    for b in range(batch):
        xb = x.reshape(seq, d_model) if batch == 1 else x[b]
        for i in range(depth):
            lam_rows = jnp.stack([lam_q1[i], lam_k1[i], lam_q2[i], lam_k2[i]]).astype(F32)
            pb = p.reshape(seq, PLE_DIM) if batch == 1 and depth == 1 else p[i, b]
            xb = _layer(i, xb, pb, g_mix[i], w_in[i], b_gates[i], g_q[i], g_k[i], lam_rows,
                        g_sub[i], w_conv_mix[i], w_att_out[i], w_conv_out[i], w_o[i], g_ffn[i],
                        w_up[i], w_ffn_conv[i], w_down[i], w_ple[i], g_ple[i], g_pg[i], w_pg[i],
                        b_pg[i])
        outs.append(xb)
    return outs[0].reshape(x.shape) if batch == 1 else jnp.stack(outs)
```

```python
import functools
import math

import jax
import jax.numpy as jnp
import numpy as np
from jax import lax
from jax.experimental import pallas as pl
from jax.experimental.pallas import tpu as pltpu

F32 = jnp.float32
BF16 = jnp.bfloat16

D_MODEL = 2048
ATT_HEADS = 8
ATT_HEAD_DIM = 64
ATT_V_DIM = 2 * ATT_HEAD_DIM
ATT_QK_WIDTH = ATT_HEADS * 2 * ATT_HEAD_DIM
ATT_WIDTH = ATT_HEADS * ATT_V_DIM
CONV_WIDTH = 1024
D_FF = 5632
PLE_DIM = 256
EPS = 1e-6

LANES = 128
CARRY_ROWS = 8
MASK_VALUE = -1e30
VMEM_LIMIT = 56 * 1024 * 1024

OFF_Q = 0
OFF_K = ATT_QK_WIDTH
OFF_V = 2 * ATT_QK_WIDTH
OFF_CB = OFF_V + ATT_WIDTH
OFF_CC = OFF_CB + CONV_WIDTH
OFF_CX = OFF_CC + CONV_WIDTH
OFF_GA = OFF_CX + CONV_WIDTH
OFF_GC = OFF_GA + D_MODEL

NT_DIMS = (((1,), (1,)), ((), ()))


def _rms_rows(xf, g):
    ms = jnp.mean(xf * xf, axis=-1, keepdims=True)
    return xf * lax.rsqrt(ms + EPS) * g


def _params(*sem):
    return pltpu.CompilerParams(dimension_semantics=sem, vmem_limit_bytes=VMEM_LIMIT)


def _resident(shape, index_map):
    return pl.BlockSpec(shape, index_map, pipeline_mode=pl.Buffered(1))


def _shift_rows(u, prev, rows):
    u1 = jnp.where(rows == 0, prev[CARRY_ROWS - 1:CARRY_ROWS], pltpu.roll(u, 1, 0))
    u2 = jnp.where(rows == 0, prev[CARRY_ROWS - 2:CARRY_ROWS - 1],
                   jnp.where(rows == 1, prev[CARRY_ROWS - 1:CARRY_ROWS], pltpu.roll(u, 2, 0)))
    return u1, u2


def _causal_conv3(u, w, prev):
    rows = lax.broadcasted_iota(jnp.int32, u.shape, 0)
    u1, u2 = _shift_rows(u, prev, rows)
    return w[0:1] * u2 + w[1:2] * u1 + w[2:3] * u


ROW_GAIN, ROW_ONES, ROW_SLOPE, ROW_HI, ROW_MID, ROW_LO, TABLE_ROWS = 0, 1, 2, 3, 4, 5, 8
V_GROUP = 2 * LANES
MXU_COLS = 256
PROJ_TILE = 512
N_QK_TILES = 2 * ATT_QK_WIDTH // PROJ_TILE
N_QKV_TILES = N_QK_TILES + ATT_WIDTH // PROJ_TILE
N_CONV_TILES = CONV_WIDTH // PROJ_TILE
QKV_OUT_TILE = 2 * PROJ_TILE
QKV_OUT_WIDTH = N_QKV_TILES * QKV_OUT_TILE


def _qk_tile(acc, tab_ref, o_ref, first_pos):
    tm = acc.shape[0]
    lane = lax.broadcasted_iota(jnp.int32, (tm, LANES), 1)
    pos = (first_pos + lax.broadcasted_iota(jnp.int32, (tm, LANES), 0)).astype(F32)
    first = lane < ATT_HEAD_DIM
    for c in range(PROJ_TILE // LANES):
        sl = slice(c * LANES, (c + 1) * LANES)
        t = acc[:, sl]
        sq = t * t
        ss0 = jnp.sum(jnp.where(first, sq, 0.0), axis=-1, keepdims=True)
        ss1 = jnp.sum(jnp.where(first, 0.0, sq), axis=-1, keepdims=True)
        r = jnp.where(first, lax.rsqrt(ss0 / ATT_HEAD_DIM + EPS), lax.rsqrt(ss1 / ATT_HEAD_DIM + EPS))
        tnorm = t * r * tab_ref[ROW_GAIN:ROW_GAIN + 1, sl]
        bias = tab_ref[ROW_SLOPE:ROW_SLOPE + 1, sl] * pos
        hi = bias.astype(BF16).astype(F32)
        rem = bias - hi
        mid = rem.astype(BF16).astype(F32)
        lo = rem - mid
        aug = (tab_ref[ROW_ONES:ROW_ONES + 1, sl] + tab_ref[ROW_HI:ROW_HI + 1, sl] * hi
               + tab_ref[ROW_MID:ROW_MID + 1, sl] * mid + tab_ref[ROW_LO:ROW_LO + 1, sl] * lo)
        comp0 = jnp.where(first, tnorm, aug)
        comp1 = jnp.where(first, pltpu.roll(tnorm, ATT_HEAD_DIM, 1), aug)
        o_ref[:, 2 * c * LANES:(2 * c + 1) * LANES] = comp0.astype(BF16)
        o_ref[:, (2 * c + 1) * LANES:(2 * c + 2) * LANES] = comp1.astype(BF16)


def _v_tile(acc, o_ref):
    tm = acc.shape[0]
    ones_lane = (lax.broadcasted_iota(jnp.int32, (tm, LANES), 1) == 0).astype(BF16)
    for a in range(PROJ_TILE // ATT_V_DIM):
        o_ref[:, a * V_GROUP:a * V_GROUP + ATT_V_DIM] = acc[:, a * ATT_V_DIM:(a + 1) * ATT_V_DIM].astype(BF16)
        o_ref[:, a * V_GROUP + ATT_V_DIM:(a + 1) * V_GROUP] = ones_lane


def _inproj_kernel(x_ref, g_ref, w_ref, wb_ref, wc_ref, wx_ref, tab_ref, cw_ref, qkv_ref, conv_ref,
                   h_ref, carry_ref, *, tm):
    i = pl.program_id(0)
    j = pl.program_id(1)

    @pl.when(j == 0)
    def _():
        h_ref[...] = _rms_rows(x_ref[...], g_ref[...]).astype(BF16)

    @pl.when(j < N_QK_TILES)
    def _():
        acc = jnp.dot(h_ref[...], w_ref[...], preferred_element_type=F32)
        _qk_tile(acc, tab_ref, qkv_ref, i * tm)

    @pl.when(jnp.logical_and(j >= N_QK_TILES, j < N_QKV_TILES))
    def _():
        _v_tile(jnp.dot(h_ref[...], w_ref[...], preferred_element_type=F32), qkv_ref)

    @pl.when(j >= N_QKV_TILES)
    def _():
        jc = j - N_QKV_TILES

        @pl.when(i == 0)
        def _():
            carry_ref[jc] = jnp.zeros(carry_ref.shape[1:], F32)

        h = h_ref[...]
        for a in range(PROJ_TILE // MXU_COLS):
            sl = slice(a * MXU_COLS, (a + 1) * MXU_COLS)
            c_b = jnp.dot(h, wb_ref[:, sl], preferred_element_type=F32)
            z = (jnp.dot(h, wc_ref[:, sl], preferred_element_type=F32)
                 * jnp.dot(h, wx_ref[:, sl], preferred_element_type=F32))
            y = _causal_conv3(z, cw_ref[:, sl], carry_ref[jc, :, sl])
            conv_ref[:, sl] = (c_b * y).astype(BF16)
            carry_ref[jc, :, sl] = z[tm - CARRY_ROWS:tm]


def _qk_table(g_q, g_k):
    n_groups = ATT_QK_WIDTH // ATT_HEAD_DIM
    slopes = jnp.asarray(2.0 ** (-8.0 * np.arange(1, ATT_HEADS + 1) / ATT_HEADS), F32)
    slope_cols = jnp.repeat(slopes, 2 * ATT_HEAD_DIM)
    lane = jnp.tile(jnp.arange(LANES), ATT_HEADS) - ATT_HEAD_DIM
    at = lambda n: (lane == n).astype(F32)
    zeros = jnp.zeros(ATT_QK_WIDTH, F32)
    both = lambda q_part, k_part: jnp.concatenate([q_part, k_part])
    rows = [None] * TABLE_ROWS
    rows[ROW_GAIN] = both(jnp.tile(g_q.astype(F32) * ATT_HEAD_DIM ** -0.5, n_groups),
                          jnp.tile(g_k.astype(F32), n_groups))
    rows[ROW_ONES] = both(at(0) + at(1) + at(2), at(3) + at(4) + at(5))
    rows[ROW_SLOPE] = both(-slope_cols, slope_cols)
    rows[ROW_HI] = both(at(3), at(0))
    rows[ROW_MID] = both(at(4), at(1))
    rows[ROW_LO] = both(at(5), at(2))
    return jnp.stack([both(zeros, zeros) if r is None else r for r in rows])


def _inproj_call(x, g_mix, w_in, table, w_conv, *, tm):
    s = x.shape[0]
    qkv_tile = lambda j: jnp.minimum(j, N_QKV_TILES - 1)
    conv_tile = lambda j: jnp.maximum(j - N_QKV_TILES, 0)
    conv_w = lambda off: pl.BlockSpec((D_MODEL, PROJ_TILE),
                                      lambda i, j: (0, off // PROJ_TILE + conv_tile(j)))
    return pl.pallas_call(
        functools.partial(_inproj_kernel, tm=tm),
        grid=(s // tm, N_QKV_TILES + N_CONV_TILES),
        in_specs=[
            pl.BlockSpec((tm, D_MODEL), lambda i, j: (i, 0)),
            pl.BlockSpec((1, D_MODEL), lambda i, j: (0, 0)),
            pl.BlockSpec((D_MODEL, PROJ_TILE), lambda i, j: (0, qkv_tile(j))),
            conv_w(OFF_CB), conv_w(OFF_CC), conv_w(OFF_CX),
            pl.BlockSpec((TABLE_ROWS, PROJ_TILE), lambda i, j: (0, jnp.minimum(j, N_QK_TILES - 1))),
            pl.BlockSpec((3, PROJ_TILE), lambda i, j: (0, conv_tile(j))),
        ],
        out_specs=[pl.BlockSpec((tm, QKV_OUT_TILE), lambda i, j: (i, qkv_tile(j))),
                   pl.BlockSpec((tm, PROJ_TILE), lambda i, j: (i, conv_tile(j)))],
        out_shape=[jax.ShapeDtypeStruct((s, QKV_OUT_WIDTH), BF16),
                   jax.ShapeDtypeStruct((s, CONV_WIDTH), BF16)],
        scratch_shapes=[pltpu.VMEM((tm, D_MODEL), BF16),
                        pltpu.VMEM((N_CONV_TILES, CARRY_ROWS, PROJ_TILE), F32)],
        compiler_params=_params("arbitrary", "arbitrary"),
        name="in_proj",
    )(x, g_mix, w_in, w_in, w_in, w_in, table, w_conv)


def _attn_kernel(lam_ref, gsub_ref, q0_ref, q1_ref, k0_ref, k1_ref, v_ref, o_ref,
                 m_ref, l_ref, acc_ref, *, t, lambda_init, side_work):
    i = pl.program_id(1)
    side_work()
    m_ref[...] = jnp.full(m_ref.shape, MASK_VALUE, F32)
    l_ref[...] = jnp.zeros(l_ref.shape, F32)
    acc_ref[...] = jnp.zeros(acc_ref.shape, F32)

    q = (q0_ref[...], q1_ref[...])
    k_refs = (k0_ref, k1_ref)
    n_chunks = t // LANES

    def tile(j, masked):
        start = pl.multiple_of(j * t, t)
        v = v_ref[pl.ds(start, t), :]
        if masked:
            keep = (lax.broadcasted_iota(jnp.int32, (t, t), 1)
                    <= lax.broadcasted_iota(jnp.int32, (t, t), 0))
        for c in range(2):
            k = k_refs[c][pl.ds(start, t), :]
            s = lax.dot_general(q[c], k, NT_DIMS, preferred_element_type=F32)
            if masked:
                s = jnp.where(keep, s, MASK_VALUE)
            chunks = [s[:, a * LANES:(a + 1) * LANES] for a in range(n_chunks)]
            m_prev = m_ref[c]
            m_new = jnp.maximum(m_prev, jnp.max(functools.reduce(jnp.maximum, chunks),
                                                axis=1, keepdims=True))
            alpha = jnp.exp(m_prev - m_new)
            ps = [jnp.exp(ch - m_new) for ch in chunks]
            l_ref[c] = alpha * l_ref[c] + functools.reduce(jnp.add, ps)
            p = jnp.concatenate([x.astype(BF16) for x in ps], axis=1)
            acc_ref[c] = alpha * acc_ref[c] + jnp.dot(p, v, preferred_element_type=F32)
            m_ref[c] = m_new

    def body(j, carry):
        tile(j, False)
        return carry

    lax.fori_loop(0, i, body, 0)
    tile(i, True)

    l0 = jnp.sum(l_ref[0], axis=1, keepdims=True)
    l1 = jnp.sum(l_ref[1], axis=1, keepdims=True)
    _attn_finish(acc_ref[0], l0, acc_ref[1], l1, lam_ref, gsub_ref, o_ref, lambda_init)


def _attn_finish(acc0, l0, acc1, l1, lam_ref, gsub_ref, o_ref, lambda_init):
    lam_rows = lam_ref[...]
    lam = (jnp.exp(jnp.sum(lam_rows[0:1] * lam_rows[1:2], axis=-1, keepdims=True))
           - jnp.exp(jnp.sum(lam_rows[2:3] * lam_rows[3:4], axis=-1, keepdims=True))
           + lambda_init)
    o = acc0 / l0 - lam * (acc1 / l1)
    o_ref[...] = (_rms_rows(o, gsub_ref[...]) * (1.0 - lambda_init)).astype(BF16)


def _attn_bounded_kernel(lam_ref, gsub_ref, q0_ref, q1_ref, k0_ref, k1_ref, v_ref, o_ref, acc_ref,
                         *, t, lambda_init, wide, side_work):
    i = pl.program_id(1)
    q = (q0_ref[...], q1_ref[...])
    k_refs = (k0_ref, k1_ref)

    def tile(first, n_tiles, diagonal_last):
        start = pl.multiple_of(first * t, t)
        n_keys = n_tiles * t
        v = v_ref[pl.ds(start, n_keys), :]
        for c in range(2):
            k = k_refs[c][pl.ds(start, n_keys), :]
            s = lax.dot_general(q[c], k, NT_DIMS, preferred_element_type=F32)
            if diagonal_last:
                keep = (lax.broadcasted_iota(jnp.int32, (t, t), 1)
                        <= lax.broadcasted_iota(jnp.int32, (t, t), 0))
                parts = [jnp.where(keep, s[:, n_keys - t:], MASK_VALUE)]
                if n_tiles > 1:
                    parts.insert(0, s[:, :n_keys - t])
                s = jnp.concatenate(parts, axis=1)
            pv = jnp.dot(jnp.exp(s).astype(BF16), v, preferred_element_type=F32)
            if diagonal_last:
                acc_ref[c] = pv
            else:
                acc_ref[c] += pv

    n_wide = i // wide
    for r in range(wide):
        @pl.when(i - n_wide * wide == r)
        def _(r=r):
            side_work()
            tile(i - r, r + 1, True)

    def wide_body(j, carry):
        tile(j * wide, wide, False)
        return carry

    lax.fori_loop(0, n_wide, wide_body, 0)

    acc0 = acc_ref[0]
    acc1 = acc_ref[1]
    _attn_finish(acc0[:, :ATT_V_DIM], acc0[:, ATT_V_DIM:ATT_V_DIM + 1],
                 acc1[:, :ATT_V_DIM], acc1[:, ATT_V_DIM:ATT_V_DIM + 1],
                 lam_ref, gsub_ref, o_ref, lambda_init)


ATT_WIDE_TILES = 4


BF16_ROWS = 16
N_ATTN_INPUTS = 7


def _with_weight_casts(attn_body, n):
    def kernel(*refs):
        ins, refs = refs[:N_ATTN_INPUTS], refs[N_ATTN_INPUTS:]
        cast_in, o_ref, cast_out, scratch = refs[:n], refs[n], refs[n + 1:2 * n + 1], refs[2 * n + 1:]

        def cast_weights():
            for src, dst in zip(cast_in, cast_out):
                dst[...] = src[...].astype(BF16)

        attn_body(*ins, o_ref, *scratch, side_work=cast_weights)

    return kernel


def _attn_call(lam_rows, g_sub, qkv, *cast_arrays, cast_jobs, t, lambda_init, bounded):
    s = qkv.shape[0]
    n_q = s // t
    n_steps = ATT_HEADS * n_q
    k_off = 2 * ATT_HEADS
    v_off = 2 * ATT_QK_WIDTH // ATT_HEAD_DIM
    if bounded:
        body = functools.partial(_attn_bounded_kernel, wide=ATT_WIDE_TILES)
        v_spec = pl.BlockSpec((s, V_GROUP), lambda h, i: (0, v_off * LANES // V_GROUP + h))
        scratch = [pltpu.VMEM((2, t, V_GROUP), F32)]
    else:
        body = _attn_kernel
        v_spec = pl.BlockSpec((s, ATT_V_DIM), lambda h, i: (0, v_off + (V_GROUP // ATT_V_DIM) * h))
        scratch = [pltpu.VMEM((2, t, LANES), F32),
                   pltpu.VMEM((2, t, LANES), F32),
                   pltpu.VMEM((2, t, ATT_V_DIM), F32)]
    cast_in_specs, cast_out_specs, cast_out_shapes = [], [], []
    for arr, (width, col_block) in zip(cast_arrays, cast_jobs):
        rows = arr.shape[0]
        col_blocks = 1
        while rows % (BF16_ROWS * n_steps // col_blocks):
            col_blocks *= 2
        block = (rows * col_blocks // n_steps, width // col_blocks)
        assert block[1] % LANES == 0 and block[1] * col_blocks == width
        cast_in_specs.append(pl.BlockSpec(
            block, lambda h, i, cb=col_blocks, c0=col_block * col_blocks:
            ((h * n_q + i) // cb, c0 + (h * n_q + i) % cb)))
        cast_out_specs.append(pl.BlockSpec(
            block, lambda h, i, cb=col_blocks: ((h * n_q + i) // cb, (h * n_q + i) % cb)))
        cast_out_shapes.append(jax.ShapeDtypeStruct((rows, width), BF16))
    return pl.pallas_call(
        _with_weight_casts(functools.partial(body, t=t, lambda_init=lambda_init), len(cast_jobs)),
        grid=(ATT_HEADS, n_q),
        in_specs=[
            pl.BlockSpec((4, ATT_HEAD_DIM), lambda h, i: (0, 0)),
            pl.BlockSpec((1, ATT_V_DIM), lambda h, i: (0, 0)),
            pl.BlockSpec((t, LANES), lambda h, i: (i, 2 * h)),
            pl.BlockSpec((t, LANES), lambda h, i: (i, 2 * h + 1)),
            pl.BlockSpec((s, LANES), lambda h, i: (0, k_off + 2 * h)),
            pl.BlockSpec((s, LANES), lambda h, i: (0, k_off + 2 * h + 1)),
            v_spec,
        ] + cast_in_specs,
        out_specs=[pl.BlockSpec((t, ATT_V_DIM), lambda h, i: (i, h))] + cast_out_specs,
        out_shape=[jax.ShapeDtypeStruct((s, ATT_WIDTH), BF16)] + cast_out_shapes,
        scratch_shapes=scratch,
        compiler_params=_params("arbitrary", "arbitrary"),
        name="diff_attn_bounded" if bounded else "diff_attn",
    )(lam_rows, g_sub, qkv, qkv, qkv, qkv, qkv, *cast_arrays)


SCORE_BOUND_LIMIT = 40.0


def _attention(lam_rows, g_sub, g_q, g_k, qkv, cast_arrays, cast_jobs, *, t, lambda_init):
    bound = (1.01 * ATT_HEAD_DIM ** 0.5
             * jnp.max(jnp.abs(g_q.astype(F32))) * jnp.max(jnp.abs(g_k.astype(F32))))
    call = functools.partial(_attn_call, cast_jobs=cast_jobs, t=t, lambda_init=lambda_init)
    return lax.cond(bound <= SCORE_BOUND_LIMIT,
                    functools.partial(call, bounded=True),
                    functools.partial(call, bounded=False),
                    lam_rows, g_sub, qkv, *cast_arrays)


def _merge_kernel(x_ref, g_ref, ya_ref, yc_ref, wga_ref, wgc_ref, bg_ref, wa_ref, wc_ref, wo_ref,
                  gffn_ref, x1_ref, h2_ref):
    xf = x_ref[...]
    h = _rms_rows(xf, g_ref[...]).astype(BF16)
    bg = bg_ref[...]
    gate_a = jax.nn.sigmoid(jnp.dot(h, wga_ref[...], preferred_element_type=F32) + bg[0:1])
    merged = gate_a * jnp.dot(ya_ref[...], wa_ref[...], preferred_element_type=F32)
    gate_c = jax.nn.sigmoid(jnp.dot(h, wgc_ref[...], preferred_element_type=F32) + bg[1:2])
    merged = merged + gate_c * jnp.dot(yc_ref[...], wc_ref[...], preferred_element_type=F32)
    x1 = xf + jnp.dot(merged.astype(BF16), wo_ref[...], preferred_element_type=F32)
    x1_ref[...] = x1
    h2_ref[...] = _rms_rows(x1, gffn_ref[...]).astype(BF16)


def _merge_call(x, g_mix, y_att, y_conv, w_ga, w_gc, b_gates, w_att_out, w_conv_out, w_o, g_ffn,
                *, tm):
    s = x.shape[0]
    const = lambda i: (0, 0)
    return pl.pallas_call(
        _merge_kernel,
        grid=(s // tm,),
        in_specs=[
            pl.BlockSpec((tm, D_MODEL), lambda i: (i, 0)),
            _resident((1, D_MODEL), const),
            pl.BlockSpec((tm, ATT_WIDTH), lambda i: (i, 0)),
            pl.BlockSpec((tm, CONV_WIDTH), lambda i: (i, 0)),
            _resident((D_MODEL, D_MODEL), const),
            _resident((D_MODEL, D_MODEL), const),
            _resident((2, D_MODEL), const),
            _resident((ATT_WIDTH, D_MODEL), const),
            _resident((CONV_WIDTH, D_MODEL), const),
            _resident((D_MODEL, D_MODEL), const),
            _resident((1, D_MODEL), const),
        ],
        out_specs=[pl.BlockSpec((tm, D_MODEL), lambda i: (i, 0)),
                   pl.BlockSpec((tm, D_MODEL), lambda i: (i, 0))],
        out_shape=[jax.ShapeDtypeStruct((s, D_MODEL), F32),
                   jax.ShapeDtypeStruct((s, D_MODEL), BF16)],
        compiler_params=_params("arbitrary"),
        name="merge",
    )(x, g_mix, y_att, y_conv, w_ga, w_gc, b_gates, w_att_out, w_conv_out, w_o, g_ffn)


FFN_SLICE = 256


def _ffn_kernel(h_ref, wg_ref, wv_ref, cwg_ref, cwv_ref, wd_ref, o_ref,
                carry_ref, pre_ref, act_ref, *, tm):
    i = pl.program_id(0)
    f = pl.program_id(1)

    @pl.when(i == 0)
    def _():
        carry_ref[f] = jnp.zeros(carry_ref.shape[1:], F32)

    @pl.when(f == 0)
    def _():
        o_ref[...] = jnp.zeros(o_ref.shape, F32)

    h = h_ref[...]
    n_slices = wg_ref.shape[1] // FFN_SLICE
    slices = [slice(a * FFN_SLICE, (a + 1) * FFN_SLICE) for a in range(n_slices)]
    for a, sl in enumerate(slices):
        for b, (w_ref, cw_ref) in enumerate(((wg_ref, cwg_ref), (wv_ref, cwv_ref))):
            pre_ref[b, a, 0:CARRY_ROWS] = carry_ref[f, b, :, sl]
            pre_ref[b, a, CARRY_ROWS:CARRY_ROWS + tm] = jnp.dot(h, w_ref[:, sl],
                                                                preferred_element_type=F32)
            carry_ref[f, b, :, sl] = pre_ref[b, a, tm:tm + CARRY_ROWS]
    for a, sl in enumerate(slices):
        u = []
        for b, cw_ref in enumerate((cwg_ref, cwv_ref)):
            w = cw_ref[:, sl]
            u.append(w[0:1] * pre_ref[b, a, CARRY_ROWS - 2:CARRY_ROWS - 2 + tm]
                     + w[1:2] * pre_ref[b, a, CARRY_ROWS - 1:CARRY_ROWS - 1 + tm]
                     + w[2:3] * pre_ref[b, a, CARRY_ROWS:CARRY_ROWS + tm])
        act_ref[a] = (u[0] * jax.nn.sigmoid(u[0]) * u[1]).astype(BF16)
    for a, sl in enumerate(slices):
        for n in range(o_ref.shape[1] // FFN_SLICE):
            cs = slice(n * FFN_SLICE, (n + 1) * FFN_SLICE)
            o_ref[:, cs] += jnp.dot(act_ref[a], wd_ref[sl, cs], preferred_element_type=F32)


def _ffn_call(h2, w_up, w_ffn_conv, w_down, *, tm, tf):
    s = h2.shape[0]
    n_f = D_FF // tf
    return pl.pallas_call(
        functools.partial(_ffn_kernel, tm=tm),
        grid=(s // tm, n_f),
        in_specs=[
            pl.BlockSpec((tm, D_MODEL), lambda i, f: (i, 0)),
            pl.BlockSpec((D_MODEL, tf), lambda i, f: (0, f)),
            pl.BlockSpec((D_MODEL, tf), lambda i, f: (0, n_f + f)),
            pl.BlockSpec((3, tf), lambda i, f: (0, f)),
            pl.BlockSpec((3, tf), lambda i, f: (0, n_f + f)),
            pl.BlockSpec((tf, D_MODEL), lambda i, f: (f, 0)),
        ],
        out_specs=pl.BlockSpec((tm, D_MODEL), lambda i, f: (i, 0)),
        out_shape=jax.ShapeDtypeStruct((s, D_MODEL), F32),
        scratch_shapes=[pltpu.VMEM((n_f, 2, CARRY_ROWS, tf), F32),
                        pltpu.VMEM((2, tf // FFN_SLICE, CARRY_ROWS + tm, FFN_SLICE), F32),
                        pltpu.VMEM((tf // FFN_SLICE, tm, FFN_SLICE), BF16)],
        compiler_params=_params("arbitrary", "arbitrary"),
        name="conv_ffn",
    )(h2, w_up, w_up, w_ffn_conv, w_ffn_conv, w_down)


def _ple_kernel(x_ref, y_ref, p_ref, wple_ref, gple_ref, gpg_ref, wpg_ref, bpg_ref, o_ref):
    xf = x_ref[...] + y_ref[...]
    pe = _rms_rows(jnp.dot(p_ref[...].astype(BF16), wple_ref[...], preferred_element_type=F32),
                   gple_ref[...])
    hn = _rms_rows(xf, gpg_ref[...]).astype(BF16)
    pg = jax.nn.sigmoid(jnp.dot(hn, wpg_ref[...], preferred_element_type=F32) + bpg_ref[...])
    o_ref[...] = xf + pg * pe


def _ple_call(x1, y_ffn, p, w_ple, g_ple, g_pg, w_pg, b_pg, *, tm):
    s = x1.shape[0]
    const = lambda i: (0, 0)
    return pl.pallas_call(
        _ple_kernel,
        grid=(s // tm,),
        in_specs=[
            pl.BlockSpec((tm, D_MODEL), lambda i: (i, 0)),
            pl.BlockSpec((tm, D_MODEL), lambda i: (i, 0)),
            pl.BlockSpec((tm, PLE_DIM), lambda i: (i, 0)),
            _resident((PLE_DIM, D_MODEL), const),
            _resident((1, D_MODEL), const),
            _resident((1, D_MODEL), const),
            _resident((D_MODEL, D_MODEL), const),
            _resident((1, D_MODEL), const),
        ],
        out_specs=pl.BlockSpec((tm, D_MODEL), lambda i: (i, 0)),
        out_shape=jax.ShapeDtypeStruct((s, D_MODEL), F32),
        compiler_params=_params("arbitrary"),
        name="ple_gate",
    )(x1, y_ffn, p, w_ple, g_ple, g_pg, w_pg, b_pg)


def _layer(depth_index, x, p, g_mix, w_in, b_gates, g_q, g_k, lam_rows, g_sub, w_conv_mix, w_att_out,
           w_conv_out, w_o, g_ffn, w_up, w_ffn_conv, w_down, w_ple, g_ple, g_pg, w_pg, b_pg):
    lambda_init = 0.8 - 0.6 * math.exp(-0.3 * depth_index)
    row = lambda a: a.reshape(1, -1).astype(F32)

    g_mix = row(g_mix)
    w_proj = w_in[:, :OFF_GA].astype(BF16)
    qkv, y_conv = _inproj_call(x, g_mix, w_proj, _qk_table(g_q, g_k), w_conv_mix.astype(F32), tm=512)
    later = [w_in, w_in, w_att_out, w_conv_out, w_o, w_up, w_down, w_ple, w_pg]
    jobs = ([(D_MODEL, OFF_GA // D_MODEL), (D_MODEL, OFF_GC // D_MODEL)]
            + [(w.shape[1], 0) for w in later[2:]])
    y_att, w_ga, w_gc, w_att_out, w_conv_out, w_o, w_up, w_down, w_ple, w_pg = _attention(
        lam_rows, row(g_sub), g_q, g_k, qkv, [w.astype(F32) for w in later], jobs,
        t=512, lambda_init=lambda_init)
    x1, h2 = _merge_call(x, g_mix, y_att, y_conv, w_ga, w_gc, b_gates.astype(F32),
                         w_att_out, w_conv_out, w_o, row(g_ffn), tm=256)
    y_ffn = _ffn_call(h2, w_up, w_ffn_conv.astype(F32), w_down, tm=1024, tf=512)
    return _ple_call(x1, y_ffn, p, w_ple, row(g_ple), row(g_pg), w_pg, row(b_pg), tm=256)


def kernel(x, p, g_mix, w_in, b_gates, g_q, g_k, lam_q1, lam_k1, lam_q2, lam_k2, g_sub, w_conv_mix,
           w_att_out, w_conv_out, w_o, g_ffn, w_up, w_ffn_conv, w_down, w_ple, g_ple, g_pg, w_pg,
           b_pg):
    batch, seq, d_model = x.shape
    depth = p.shape[0]
    outs = []
    for b in range(batch):
        xb = x.reshape(seq, d_model) if batch == 1 else x[b]
        for i in range(depth):
            lam_rows = jnp.stack([lam_q1[i], lam_k1[i], lam_q2[i], lam_k2[i]]).astype(F32)
            pb = p.reshape(seq, PLE_DIM) if batch == 1 and depth == 1 else p[i, b]
            xb = _layer(i, xb, pb, g_mix[i], w_in[i], b_gates[i], g_q[i], g_k[i], lam_rows,
                        g_sub[i], w_conv_mix[i], w_att_out[i], w_conv_out[i], w_o[i], g_ffn[i],
                        w_up[i], w_ffn_conv[i], w_down[i], w_ple[i], g_ple[i], g_pg[i], w_pg[i],
                        b_pg[i])
        outs.append(xb)
    return outs[0].reshape(x.shape) if batch == 1 else jnp.stack(outs)
```

```python
import functools
import math

import jax
import jax.numpy as jnp
import numpy as np
from jax import lax
from jax.experimental import pallas as pl
from jax.experimental.pallas import tpu as pltpu

F32 = jnp.float32
BF16 = jnp.bfloat16

D_MODEL = 2048
ATT_HEADS = 8
ATT_HEAD_DIM = 64
ATT_V_DIM = 2 * ATT_HEAD_DIM
ATT_QK_WIDTH = ATT_HEADS * 2 * ATT_HEAD_DIM
ATT_WIDTH = ATT_HEADS * ATT_V_DIM
CONV_WIDTH = 1024
D_FF = 5632
PLE_DIM = 256
EPS = 1e-6

LANES = 128
CARRY_ROWS = 8
MASK_VALUE = -1e30
VMEM_LIMIT = 56 * 1024 * 1024

OFF_Q = 0
OFF_K = ATT_QK_WIDTH
OFF_V = 2 * ATT_QK_WIDTH
OFF_CB = OFF_V + ATT_WIDTH
OFF_CC = OFF_CB + CONV_WIDTH
OFF_CX = OFF_CC + CONV_WIDTH
OFF_GA = OFF_CX + CONV_WIDTH
OFF_GC = OFF_GA + D_MODEL

NT_DIMS = (((1,), (1,)), ((), ()))


def _rms_rows(xf, g):
    ms = jnp.mean(xf * xf, axis=-1, keepdims=True)
    return xf * lax.rsqrt(ms + EPS) * g


def _params(*sem):
    return pltpu.CompilerParams(dimension_semantics=sem, vmem_limit_bytes=VMEM_LIMIT)


def _resident(shape, index_map):
    return pl.BlockSpec(shape, index_map, pipeline_mode=pl.Buffered(1))


def _shift_rows(u, prev, rows):
    u1 = jnp.where(rows == 0, prev[CARRY_ROWS - 1:CARRY_ROWS], pltpu.roll(u, 1, 0))
    u2 = jnp.where(rows == 0, prev[CARRY_ROWS - 2:CARRY_ROWS - 1],
                   jnp.where(rows == 1, prev[CARRY_ROWS - 1:CARRY_ROWS], pltpu.roll(u, 2, 0)))
    return u1, u2


def _causal_conv3(u, w, prev):
    rows = lax.broadcasted_iota(jnp.int32, u.shape, 0)
    u1, u2 = _shift_rows(u, prev, rows)
    return w[0:1] * u2 + w[1:2] * u1 + w[2:3] * u


ROW_GAIN, ROW_ONES, ROW_SLOPE, ROW_HI, ROW_MID, ROW_LO, TABLE_ROWS = 0, 1, 2, 3, 4, 5, 8
V_GROUP = 2 * LANES
MXU_COLS = 256
PROJ_TILE = 512
N_QK_TILES = 2 * ATT_QK_WIDTH // PROJ_TILE
N_QKV_TILES = N_QK_TILES + ATT_WIDTH // PROJ_TILE
N_CONV_TILES = CONV_WIDTH // PROJ_TILE
N_PROJ_TILES = N_QKV_TILES + 3 * N_CONV_TILES
TILE_V0 = N_QK_TILES
TILE_CB0 = N_QKV_TILES
TILE_CX0 = N_QKV_TILES + 2 * N_CONV_TILES
QKV_OUT_TILE = 2 * PROJ_TILE
QKV_OUT_WIDTH = N_QKV_TILES * QKV_OUT_TILE


def _qk_tile(acc, tab_ref, o_ref, first_pos, col0=0):
    tm = acc.shape[0]
    lane = lax.broadcasted_iota(jnp.int32, (tm, LANES), 1)
    pos = (first_pos + lax.broadcasted_iota(jnp.int32, (tm, LANES), 0)).astype(F32)
    first = lane < ATT_HEAD_DIM
    for c in range(acc.shape[1] // LANES):
        t = acc[:, c * LANES:(c + 1) * LANES]
        sl = slice(col0 + c * LANES, col0 + (c + 1) * LANES)
        sq = t * t
        ss0 = jnp.sum(jnp.where(first, sq, 0.0), axis=-1, keepdims=True)
        ss1 = jnp.sum(jnp.where(first, 0.0, sq), axis=-1, keepdims=True)
        r = jnp.where(first, lax.rsqrt(ss0 / ATT_HEAD_DIM + EPS), lax.rsqrt(ss1 / ATT_HEAD_DIM + EPS))
        tnorm = t * r * tab_ref[ROW_GAIN:ROW_GAIN + 1, sl]
        bias = tab_ref[ROW_SLOPE:ROW_SLOPE + 1, sl] * pos
        hi = bias.astype(BF16).astype(F32)
        rem = bias - hi
        mid = rem.astype(BF16).astype(F32)
        lo = rem - mid
        aug = (tab_ref[ROW_ONES:ROW_ONES + 1, sl] + tab_ref[ROW_HI:ROW_HI + 1, sl] * hi
               + tab_ref[ROW_MID:ROW_MID + 1, sl] * mid + tab_ref[ROW_LO:ROW_LO + 1, sl] * lo)
        comp0 = jnp.where(first, tnorm, aug)
        comp1 = jnp.where(first, pltpu.roll(tnorm, ATT_HEAD_DIM, 1), aug)
        out0 = 2 * (col0 + c * LANES)
        o_ref[:, out0:out0 + LANES] = comp0.astype(BF16)
        o_ref[:, out0 + LANES:out0 + 2 * LANES] = comp1.astype(BF16)


def _v_tile(acc, o_ref):
    tm = acc.shape[0]
    ones = jnp.ones((tm, V_GROUP - ATT_V_DIM), BF16)
    for a in range(PROJ_TILE // ATT_V_DIM):
        o_ref[:, a * V_GROUP:a * V_GROUP + ATT_V_DIM] = acc[:, a * ATT_V_DIM:(a + 1) * ATT_V_DIM].astype(BF16)
        o_ref[:, a * V_GROUP + ATT_V_DIM:(a + 1) * V_GROUP] = ones


def _inproj_kernel(x_ref, g_ref, w_ref, tab_ref, cw_ref, qkv_ref, conv_ref,
                   h_ref, carry_ref, park_ref, hold_ref, *, tm):
    i = pl.program_id(0)
    j = pl.program_id(1)

    def project(dst_ref, slot):
        dst_ref[slot] = jnp.dot(h_ref[...], w_ref[...], preferred_element_type=F32)

    def finish_conv(jc, cx_slot):
        z = hold_ref[N_CONV_TILES + jc] * park_ref[cx_slot]
        y = _causal_conv3(z, cw_ref[...], carry_ref[jc])
        conv_ref[...] = (hold_ref[jc] * y).astype(BF16)
        carry_ref[jc] = z[tm - CARRY_ROWS:tm]

    @pl.when(j == 0)
    def _():
        @pl.when(i == 0)
        def _():
            carry_ref[...] = jnp.zeros(carry_ref.shape, F32)

        h_ref[...] = _rms_rows(x_ref[...], g_ref[...]).astype(BF16)
        project(park_ref, 0)

    @pl.when(jnp.logical_and(j >= 1, j <= TILE_V0))
    def _():
        _qk_tile(park_ref[(j - 1) % 2], tab_ref, qkv_ref, i * tm)
        project(park_ref, j % 2)

    @pl.when(j == TILE_V0 + 1)
    def _():
        _v_tile(park_ref[TILE_V0 % 2], qkv_ref)
        project(park_ref, (TILE_V0 + 1) % 2)

    @pl.when(j == TILE_CB0)
    def _():
        _v_tile(park_ref[(TILE_V0 + 1) % 2], qkv_ref)
        project(hold_ref, 0)

    @pl.when(jnp.logical_and(j > TILE_CB0, j < TILE_CX0))
    def _():
        project(hold_ref, j - TILE_CB0)

    @pl.when(j == TILE_CX0)
    def _():
        project(park_ref, 0)

    @pl.when(j == TILE_CX0 + 1)
    def _():
        finish_conv(0, 0)
        project(park_ref, 1)

    @pl.when(j == TILE_CX0 + 2)
    def _():
        finish_conv(1, 1)


def _qk_table(g_q, g_k):
    n_groups = ATT_QK_WIDTH // ATT_HEAD_DIM
    slopes = jnp.asarray(2.0 ** (-8.0 * np.arange(1, ATT_HEADS + 1) / ATT_HEADS), F32)
    slope_cols = jnp.repeat(slopes, 2 * ATT_HEAD_DIM)
    lane = jnp.tile(jnp.arange(LANES), ATT_HEADS) - ATT_HEAD_DIM
    at = lambda n: (lane == n).astype(F32)
    zeros = jnp.zeros(ATT_QK_WIDTH, F32)
    both = lambda q_part, k_part: jnp.concatenate([q_part, k_part])
    rows = [None] * TABLE_ROWS
    rows[ROW_GAIN] = both(jnp.tile(g_q.astype(F32) * ATT_HEAD_DIM ** -0.5, n_groups),
                          jnp.tile(g_k.astype(F32), n_groups))
    rows[ROW_ONES] = both(at(0) + at(1) + at(2), at(3) + at(4) + at(5))
    rows[ROW_SLOPE] = both(-slope_cols, slope_cols)
    rows[ROW_HI] = both(at(3), at(0))
    rows[ROW_MID] = both(at(4), at(1))
    rows[ROW_LO] = both(at(5), at(2))
    return jnp.stack([both(zeros, zeros) if r is None else r for r in rows])


def _inproj_call(x, g_mix, w_in, table, w_conv, *, tm):
    s = x.shape[0]
    finished = lambda j, first, count: jnp.clip(j - 1 - first, 0, count - 1)
    return pl.pallas_call(
        functools.partial(_inproj_kernel, tm=tm),
        grid=(s // tm, N_PROJ_TILES + 1),
        in_specs=[
            pl.BlockSpec((tm, D_MODEL), lambda i, j: (i, 0)),
            pl.BlockSpec((1, D_MODEL), lambda i, j: (0, 0)),
            pl.BlockSpec((D_MODEL, PROJ_TILE), lambda i, j: (0, jnp.minimum(j, N_PROJ_TILES - 1))),
            pl.BlockSpec((TABLE_ROWS, PROJ_TILE), lambda i, j: (0, finished(j, 0, N_QK_TILES))),
            pl.BlockSpec((3, PROJ_TILE), lambda i, j: (0, finished(j, TILE_CX0, N_CONV_TILES))),
        ],
        out_specs=[pl.BlockSpec((tm, QKV_OUT_TILE), lambda i, j: (i, finished(j, 0, N_QKV_TILES))),
                   pl.BlockSpec((tm, PROJ_TILE),
                                lambda i, j: (i, finished(j, TILE_CX0, N_CONV_TILES)))],
        out_shape=[jax.ShapeDtypeStruct((s, QKV_OUT_WIDTH), BF16),
                   jax.ShapeDtypeStruct((s, CONV_WIDTH), BF16)],
        scratch_shapes=[pltpu.VMEM((tm, D_MODEL), BF16),
                        pltpu.VMEM((N_CONV_TILES, CARRY_ROWS, PROJ_TILE), F32),
                        pltpu.VMEM((2, tm, PROJ_TILE), F32),
                        pltpu.VMEM((2 * N_CONV_TILES, tm, PROJ_TILE), F32)],
        compiler_params=_params("arbitrary", "arbitrary"),
        name="in_proj",
    )(x, g_mix, w_in, table, w_conv)


def _attn_kernel(lam_ref, gsub_ref, q0_ref, q1_ref, k0_ref, k1_ref, v_ref, o_ref,
                 m_ref, l_ref, acc_ref, *, t, lambda_init, side_work):
    i = pl.program_id(1)
    side_work()
    m_ref[...] = jnp.full(m_ref.shape, MASK_VALUE, F32)
    l_ref[...] = jnp.zeros(l_ref.shape, F32)
    acc_ref[...] = jnp.zeros(acc_ref.shape, F32)

    q = (q0_ref[...], q1_ref[...])
    k_refs = (k0_ref, k1_ref)
    n_chunks = t // LANES

    def tile(j, masked):
        start = pl.multiple_of(j * t, t)
        v = v_ref[pl.ds(start, t), :]
        if masked:
            keep = (lax.broadcasted_iota(jnp.int32, (t, t), 1)
                    <= lax.broadcasted_iota(jnp.int32, (t, t), 0))
        for c in range(2):
            k = k_refs[c][pl.ds(start, t), :]
            s = lax.dot_general(q[c], k, NT_DIMS, preferred_element_type=F32)
            if masked:
                s = jnp.where(keep, s, MASK_VALUE)
            chunks = [s[:, a * LANES:(a + 1) * LANES] for a in range(n_chunks)]
            m_prev = m_ref[c]
            m_new = jnp.maximum(m_prev, jnp.max(functools.reduce(jnp.maximum, chunks),
                                                axis=1, keepdims=True))
            alpha = jnp.exp(m_prev - m_new)
            ps = [jnp.exp(ch - m_new) for ch in chunks]
            l_ref[c] = alpha * l_ref[c] + functools.reduce(jnp.add, ps)
            p = jnp.concatenate([x.astype(BF16) for x in ps], axis=1)
            acc_ref[c] = alpha * acc_ref[c] + jnp.dot(p, v, preferred_element_type=F32)
            m_ref[c] = m_new

    def body(j, carry):
        tile(j, False)
        return carry

    lax.fori_loop(0, i, body, 0)
    tile(i, True)

    l0 = jnp.sum(l_ref[0], axis=1, keepdims=True)
    l1 = jnp.sum(l_ref[1], axis=1, keepdims=True)
    _attn_finish(acc_ref[0], l0, acc_ref[1], l1, lam_ref, gsub_ref, o_ref, lambda_init)


def _attn_finish(acc0, l0, acc1, l1, lam_ref, gsub_ref, o_ref, lambda_init):
    lam_rows = lam_ref[...]
    lam = (jnp.exp(jnp.sum(lam_rows[0:1] * lam_rows[1:2], axis=-1, keepdims=True))
           - jnp.exp(jnp.sum(lam_rows[2:3] * lam_rows[3:4], axis=-1, keepdims=True))
           + lambda_init)
    o = acc0 / l0 - lam * (acc1 / l1)
    o_ref[...] = (_rms_rows(o, gsub_ref[...]) * (1.0 - lambda_init)).astype(BF16)


def _attn_bounded_kernel(lam_ref, gsub_ref, q0_ref, q1_ref, k0_ref, k1_ref, v_ref, o_ref, acc_ref,
                         *, t, lambda_init, wide, side_work):
    i = pl.program_id(1)
    q = (q0_ref[...], q1_ref[...])
    k_refs = (k0_ref, k1_ref)

    def tile(first, n_tiles, diagonal_last):
        start = pl.multiple_of(first * t, t)
        n_keys = n_tiles * t
        v = v_ref[pl.ds(start, n_keys), :]
        for c in range(2):
            k = k_refs[c][pl.ds(start, n_keys), :]
            s = lax.dot_general(q[c], k, NT_DIMS, preferred_element_type=F32)
            if diagonal_last:
                keep = (lax.broadcasted_iota(jnp.int32, (t, t), 1)
                        <= lax.broadcasted_iota(jnp.int32, (t, t), 0))
                parts = [jnp.where(keep, s[:, n_keys - t:], MASK_VALUE)]
                if n_tiles > 1:
                    parts.insert(0, s[:, :n_keys - t])
                s = jnp.concatenate(parts, axis=1)
            pv = jnp.dot(jnp.exp(s).astype(BF16), v, preferred_element_type=F32)
            if diagonal_last:
                acc_ref[c] = pv
            else:
                acc_ref[c] += pv

    n_wide = i // wide
    for r in range(wide):
        @pl.when(i - n_wide * wide == r)
        def _(r=r):
            side_work()
            tile(i - r, r + 1, True)

    def wide_body(j, carry):
        tile(j * wide, wide, False)
        return carry

    lax.fori_loop(0, n_wide, wide_body, 0)

    acc0 = acc_ref[0]
    acc1 = acc_ref[1]
    _attn_finish(acc0[:, :ATT_V_DIM], acc0[:, ATT_V_DIM:],
                 acc1[:, :ATT_V_DIM], acc1[:, ATT_V_DIM:],
                 lam_ref, gsub_ref, o_ref, lambda_init)


ATT_WIDE_TILES = 4


BF16_ROWS = 16
N_ATTN_INPUTS = 7


def _with_weight_casts(attn_body, n):
    def kernel(*refs):
        ins, refs = refs[:N_ATTN_INPUTS], refs[N_ATTN_INPUTS:]
        cast_in, o_ref, cast_out, scratch = refs[:n], refs[n], refs[n + 1:2 * n + 1], refs[2 * n + 1:]

        def cast_weights():
            for src, dst in zip(cast_in, cast_out):
                dst[...] = src[...].astype(BF16)

        attn_body(*ins, o_ref, *scratch, side_work=cast_weights)

    return kernel


def _attn_call(lam_rows, g_sub, qkv, *cast_arrays, cast_jobs, t, lambda_init, bounded):
    s = qkv.shape[0]
    n_q = s // t
    n_steps = ATT_HEADS * n_q
    k_off = 2 * ATT_HEADS
    v_off = 2 * ATT_QK_WIDTH // ATT_HEAD_DIM
    if bounded:
        body = functools.partial(_attn_bounded_kernel, wide=ATT_WIDE_TILES)
        v_spec = pl.BlockSpec((s, V_GROUP), lambda h, i: (0, v_off * LANES // V_GROUP + h))
        scratch = [pltpu.VMEM((2, t, V_GROUP), F32)]
    else:
        body = _attn_kernel
        v_spec = pl.BlockSpec((s, ATT_V_DIM), lambda h, i: (0, v_off + (V_GROUP // ATT_V_DIM) * h))
        scratch = [pltpu.VMEM((2, t, LANES), F32),
                   pltpu.VMEM((2, t, LANES), F32),
                   pltpu.VMEM((2, t, ATT_V_DIM), F32)]
    cast_in_specs, cast_out_specs, cast_out_shapes = [], [], []
    for arr, (width, col_block) in zip(cast_arrays, cast_jobs):
        rows = arr.shape[0]
        col_blocks = 1
        while rows % (BF16_ROWS * n_steps // col_blocks):
            col_blocks *= 2
        block = (rows * col_blocks // n_steps, width // col_blocks)
        assert block[1] % LANES == 0 and block[1] * col_blocks == width
        cast_in_specs.append(pl.BlockSpec(
            block, lambda h, i, cb=col_blocks, c0=col_block * col_blocks:
            ((h * n_q + i) // cb, c0 + (h * n_q + i) % cb)))
        cast_out_specs.append(pl.BlockSpec(
            block, lambda h, i, cb=col_blocks: ((h * n_q + i) // cb, (h * n_q + i) % cb)))
        cast_out_shapes.append(jax.ShapeDtypeStruct((rows, width), BF16))
    return pl.pallas_call(
        _with_weight_casts(functools.partial(body, t=t, lambda_init=lambda_init), len(cast_jobs)),
        grid=(ATT_HEADS, n_q),
        in_specs=[
            pl.BlockSpec((4, ATT_HEAD_DIM), lambda h, i: (0, 0)),
            pl.BlockSpec((1, ATT_V_DIM), lambda h, i: (0, 0)),
            pl.BlockSpec((t, LANES), lambda h, i: (i, 2 * h)),
            pl.BlockSpec((t, LANES), lambda h, i: (i, 2 * h + 1)),
            pl.BlockSpec((s, LANES), lambda h, i: (0, k_off + 2 * h)),
            pl.BlockSpec((s, LANES), lambda h, i: (0, k_off + 2 * h + 1)),
            v_spec,
        ] + cast_in_specs,
        out_specs=[pl.BlockSpec((t, ATT_V_DIM), lambda h, i: (i, h))] + cast_out_specs,
        out_shape=[jax.ShapeDtypeStruct((s, ATT_WIDTH), BF16)] + cast_out_shapes,
        scratch_shapes=scratch,
        compiler_params=_params("arbitrary", "arbitrary"),
        name="diff_attn_bounded" if bounded else "diff_attn",
    )(lam_rows, g_sub, qkv, qkv, qkv, qkv, qkv, *cast_arrays)


SCORE_BOUND_LIMIT = 40.0


def _attention(lam_rows, g_sub, g_q, g_k, qkv, cast_arrays, cast_jobs, *, t, lambda_init):
    bound = (1.01 * ATT_HEAD_DIM ** 0.5
             * jnp.max(jnp.abs(g_q.astype(F32))) * jnp.max(jnp.abs(g_k.astype(F32))))
    call = functools.partial(_attn_call, cast_jobs=cast_jobs, t=t, lambda_init=lambda_init)
    return lax.cond(bound <= SCORE_BOUND_LIMIT,
                    functools.partial(call, bounded=True),
                    functools.partial(call, bounded=False),
                    lam_rows, g_sub, qkv, *cast_arrays)


def _merge_kernel(x_ref, g_ref, ya_ref, yc_ref, wga_ref, wgc_ref, bg_ref, wa_ref, wc_ref, wo_ref,
                  gffn_ref, x1_ref, h2_ref):
    xf = x_ref[...]
    h = _rms_rows(xf, g_ref[...]).astype(BF16)
    bg = bg_ref[...]
    gate_a = jax.nn.sigmoid(jnp.dot(h, wga_ref[...], preferred_element_type=F32) + bg[0:1])
    merged = gate_a * jnp.dot(ya_ref[...], wa_ref[...], preferred_element_type=F32)
    gate_c = jax.nn.sigmoid(jnp.dot(h, wgc_ref[...], preferred_element_type=F32) + bg[1:2])
    merged = merged + gate_c * jnp.dot(yc_ref[...], wc_ref[...], preferred_element_type=F32)
    x1 = xf + jnp.dot(merged.astype(BF16), wo_ref[...], preferred_element_type=F32)
    x1_ref[...] = x1
    h2_ref[...] = _rms_rows(x1, gffn_ref[...]).astype(BF16)


def _merge_call(x, g_mix, y_att, y_conv, w_ga, w_gc, b_gates, w_att_out, w_conv_out, w_o, g_ffn,
                *, tm):
    s = x.shape[0]
    const = lambda i: (0, 0)
    return pl.pallas_call(
        _merge_kernel,
        grid=(s // tm,),
        in_specs=[
            pl.BlockSpec((tm, D_MODEL), lambda i: (i, 0)),
            _resident((1, D_MODEL), const),
            pl.BlockSpec((tm, ATT_WIDTH), lambda i: (i, 0)),
            pl.BlockSpec((tm, CONV_WIDTH), lambda i: (i, 0)),
            _resident((D_MODEL, D_MODEL), const),
            _resident((D_MODEL, D_MODEL), const),
            _resident((2, D_MODEL), const),
            _resident((ATT_WIDTH, D_MODEL), const),
            _resident((CONV_WIDTH, D_MODEL), const),
            _resident((D_MODEL, D_MODEL), const),
            _resident((1, D_MODEL), const),
        ],
        out_specs=[pl.BlockSpec((tm, D_MODEL), lambda i: (i, 0)),
                   pl.BlockSpec((tm, D_MODEL), lambda i: (i, 0))],
        out_shape=[jax.ShapeDtypeStruct((s, D_MODEL), F32),
                   jax.ShapeDtypeStruct((s, D_MODEL), BF16)],
        compiler_params=_params("arbitrary"),
        name="merge",
    )(x, g_mix, y_att, y_conv, w_ga, w_gc, b_gates, w_att_out, w_conv_out, w_o, g_ffn)


FFN_SLICE = 256


def _ffn_kernel(h_ref, wg_ref, wv_ref, cwg_ref, cwv_ref, wd_ref, o_ref,
                carry_ref, pre_ref, act_ref, *, tm):
    i = pl.program_id(0)
    f = pl.program_id(1)

    @pl.when(i == 0)
    def _():
        carry_ref[f] = jnp.zeros(carry_ref.shape[1:], F32)

    @pl.when(f == 0)
    def _():
        o_ref[...] = jnp.zeros(o_ref.shape, F32)

    h = h_ref[...]
    n_slices = wg_ref.shape[1] // FFN_SLICE
    slices = [slice(a * FFN_SLICE, (a + 1) * FFN_SLICE) for a in range(n_slices)]
    for a, sl in enumerate(slices):
        for b, (w_ref, cw_ref) in enumerate(((wg_ref, cwg_ref), (wv_ref, cwv_ref))):
            pre_ref[b, a, 0:CARRY_ROWS] = carry_ref[f, b, :, sl]
            pre_ref[b, a, CARRY_ROWS:CARRY_ROWS + tm] = jnp.dot(h, w_ref[:, sl],
                                                                preferred_element_type=F32)
            carry_ref[f, b, :, sl] = pre_ref[b, a, tm:tm + CARRY_ROWS]
    for a, sl in enumerate(slices):
        u = []
        for b, cw_ref in enumerate((cwg_ref, cwv_ref)):
            w = cw_ref[:, sl]
            u.append(w[0:1] * pre_ref[b, a, CARRY_ROWS - 2:CARRY_ROWS - 2 + tm]
                     + w[1:2] * pre_ref[b, a, CARRY_ROWS - 1:CARRY_ROWS - 1 + tm]
                     + w[2:3] * pre_ref[b, a, CARRY_ROWS:CARRY_ROWS + tm])
        act_ref[a] = (u[0] * jax.nn.sigmoid(u[0]) * u[1]).astype(BF16)
    for a, sl in enumerate(slices):
        for n in range(o_ref.shape[1] // FFN_SLICE):
            cs = slice(n * FFN_SLICE, (n + 1) * FFN_SLICE)
            o_ref[:, cs] += jnp.dot(act_ref[a], wd_ref[sl, cs], preferred_element_type=F32)


def _ffn_call(h2, w_up, w_ffn_conv, w_down, *, tm, tf):
    s = h2.shape[0]
    n_f = D_FF // tf
    return pl.pallas_call(
        functools.partial(_ffn_kernel, tm=tm),
        grid=(s // tm, n_f),
        in_specs=[
            pl.BlockSpec((tm, D_MODEL), lambda i, f: (i, 0)),
            pl.BlockSpec((D_MODEL, tf), lambda i, f: (0, f)),
            pl.BlockSpec((D_MODEL, tf), lambda i, f: (0, n_f + f)),
            pl.BlockSpec((3, tf), lambda i, f: (0, f)),
            pl.BlockSpec((3, tf), lambda i, f: (0, n_f + f)),
            pl.BlockSpec((tf, D_MODEL), lambda i, f: (f, 0)),
        ],
        out_specs=pl.BlockSpec((tm, D_MODEL), lambda i, f: (i, 0)),
        out_shape=jax.ShapeDtypeStruct((s, D_MODEL), F32),
        scratch_shapes=[pltpu.VMEM((n_f, 2, CARRY_ROWS, tf), F32),
                        pltpu.VMEM((2, tf // FFN_SLICE, CARRY_ROWS + tm, FFN_SLICE), F32),
                        pltpu.VMEM((tf // FFN_SLICE, tm, FFN_SLICE), BF16)],
        compiler_params=_params("arbitrary", "arbitrary"),
        name="conv_ffn",
    )(h2, w_up, w_up, w_ffn_conv, w_ffn_conv, w_down)


def _ple_kernel(x_ref, y_ref, p_ref, wple_ref, gple_ref, gpg_ref, wpg_ref, bpg_ref, o_ref):
    xf = x_ref[...] + y_ref[...]
    pe = _rms_rows(jnp.dot(p_ref[...].astype(BF16), wple_ref[...], preferred_element_type=F32),
                   gple_ref[...])
    hn = _rms_rows(xf, gpg_ref[...]).astype(BF16)
    pg = jax.nn.sigmoid(jnp.dot(hn, wpg_ref[...], preferred_element_type=F32) + bpg_ref[...])
    o_ref[...] = xf + pg * pe


def _ple_call(x1, y_ffn, p, w_ple, g_ple, g_pg, w_pg, b_pg, *, tm):
    s = x1.shape[0]
    const = lambda i: (0, 0)
    return pl.pallas_call(
        _ple_kernel,
        grid=(s // tm,),
        in_specs=[
            pl.BlockSpec((tm, D_MODEL), lambda i: (i, 0)),
            pl.BlockSpec((tm, D_MODEL), lambda i: (i, 0)),
            pl.BlockSpec((tm, PLE_DIM), lambda i: (i, 0)),
            _resident((PLE_DIM, D_MODEL), const),
            _resident((1, D_MODEL), const),
            _resident((1, D_MODEL), const),
            _resident((D_MODEL, D_MODEL), const),
            _resident((1, D_MODEL), const),
        ],
        out_specs=pl.BlockSpec((tm, D_MODEL), lambda i: (i, 0)),
        out_shape=jax.ShapeDtypeStruct((s, D_MODEL), F32),
        compiler_params=_params("arbitrary"),
        name="ple_gate",
    )(x1, y_ffn, p, w_ple, g_ple, g_pg, w_pg, b_pg)


def _layer(depth_index, x, p, g_mix, w_in, b_gates, g_q, g_k, lam_rows, g_sub, w_conv_mix, w_att_out,
           w_conv_out, w_o, g_ffn, w_up, w_ffn_conv, w_down, w_ple, g_ple, g_pg, w_pg, b_pg):
    lambda_init = 0.8 - 0.6 * math.exp(-0.3 * depth_index)
    row = lambda a: a.reshape(1, -1).astype(F32)

    g_mix = row(g_mix)
    w_proj = w_in[:, :OFF_GA].astype(BF16)
    qkv, y_conv = _inproj_call(x, g_mix, w_proj, _qk_table(g_q, g_k), w_conv_mix.astype(F32), tm=1024)
    later = [w_in, w_in, w_att_out, w_conv_out, w_o, w_up, w_down, w_ple, w_pg]
    jobs = ([(D_MODEL, OFF_GA // D_MODEL), (D_MODEL, OFF_GC // D_MODEL)]
            + [(w.shape[1], 0) for w in later[2:]])
    y_att, w_ga, w_gc, w_att_out, w_conv_out, w_o, w_up, w_down, w_ple, w_pg = _attention(
        lam_rows, row(g_sub), g_q, g_k, qkv, [w.astype(F32) for w in later], jobs,
        t=512, lambda_init=lambda_init)
    x1, h2 = _merge_call(x, g_mix, y_att, y_conv, w_ga, w_gc, b_gates.astype(F32),
                         w_att_out, w_conv_out, w_o, row(g_ffn), tm=256)
    y_ffn = _ffn_call(h2, w_up, w_ffn_conv.astype(F32), w_down, tm=1024, tf=512)
    return _ple_call(x1, y_ffn, p, w_ple, row(g_ple), row(g_pg), w_pg, row(b_pg), tm=512)


def kernel(x, p, g_mix, w_in, b_gates, g_q, g_k, lam_q1, lam_k1, lam_q2, lam_k2, g_sub, w_conv_mix,
           w_att_out, w_conv_out, w_o, g_ffn, w_up, w_ffn_conv, w_down, w_ple, g_ple, g_pg, w_pg,
           b_pg):
    batch, seq, d_model = x.shape
    depth = p.shape[0]
    outs = []
    for b in range(batch):
        xb = x.reshape(seq, d_model) if batch == 1 else x[b]
        for i in range(depth):
            lam_rows = jnp.stack([lam_q1[i], lam_k1[i], lam_q2[i], lam_k2[i]]).astype(F32)
            pb = p.reshape(seq, PLE_DIM) if batch == 1 and depth == 1 else p[i, b]
            xb = _layer(i, xb, pb, g_mix[i], w_in[i], b_gates[i], g_q[i], g_k[i], lam_rows,
                        g_sub[i], w_conv_mix[i], w_att_out[i], w_conv_out[i], w_o[i], g_ffn[i],
                        w_up[i], w_ffn_conv[i], w_down[i], w_ple[i], g_ple[i], g_pg[i], w_pg[i],
                        b_pg[i])
        outs.append(xb)
    return outs[0].reshape(x.shape) if batch == 1 else jnp.stack(outs)
```

```python
import functools
import math

import jax
import jax.numpy as jnp
import numpy as np
from jax import lax
from jax.experimental import pallas as pl
from jax.experimental.pallas import tpu as pltpu

F32 = jnp.float32
BF16 = jnp.bfloat16

D_MODEL = 2048
ATT_HEADS = 8
ATT_HEAD_DIM = 64
ATT_V_DIM = 2 * ATT_HEAD_DIM
ATT_QK_WIDTH = ATT_HEADS * 2 * ATT_HEAD_DIM
ATT_WIDTH = ATT_HEADS * ATT_V_DIM
CONV_WIDTH = 1024
D_FF = 5632
PLE_DIM = 256
EPS = 1e-6

LANES = 128
CARRY_ROWS = 8
MASK_VALUE = -1e30
VMEM_LIMIT = 56 * 1024 * 1024

OFF_Q = 0
OFF_K = ATT_QK_WIDTH
OFF_V = 2 * ATT_QK_WIDTH
OFF_CB = OFF_V + ATT_WIDTH
OFF_CC = OFF_CB + CONV_WIDTH
OFF_CX = OFF_CC + CONV_WIDTH
OFF_GA = OFF_CX + CONV_WIDTH
OFF_GC = OFF_GA + D_MODEL

NT_DIMS = (((1,), (1,)), ((), ()))


def _rms_rows(xf, g):
    ms = jnp.mean(xf * xf, axis=-1, keepdims=True)
    return xf * lax.rsqrt(ms + EPS) * g


def _params(*sem):
    return pltpu.CompilerParams(dimension_semantics=sem, vmem_limit_bytes=VMEM_LIMIT)


def _resident(shape, index_map):
    return pl.BlockSpec(shape, index_map, pipeline_mode=pl.Buffered(1))


def _shift_rows(u, prev, rows):
    u1 = jnp.where(rows == 0, prev[CARRY_ROWS - 1:CARRY_ROWS], pltpu.roll(u, 1, 0))
    u2 = jnp.where(rows == 0, prev[CARRY_ROWS - 2:CARRY_ROWS - 1],
                   jnp.where(rows == 1, prev[CARRY_ROWS - 1:CARRY_ROWS], pltpu.roll(u, 2, 0)))
    return u1, u2


def _causal_conv3(u, w, prev):
    rows = lax.broadcasted_iota(jnp.int32, u.shape, 0)
    u1, u2 = _shift_rows(u, prev, rows)
    return w[0:1] * u2 + w[1:2] * u1 + w[2:3] * u


ROW_GAIN, ROW_ONES, ROW_SLOPE, ROW_HI, ROW_MID, ROW_LO, TABLE_ROWS = 0, 1, 2, 3, 4, 5, 8
V_GROUP = 2 * LANES
MXU_COLS = 256
PROJ_TILE = 512
N_QK_TILES = 2 * ATT_QK_WIDTH // PROJ_TILE
N_QKV_TILES = N_QK_TILES + ATT_WIDTH // PROJ_TILE
N_CONV_TILES = CONV_WIDTH // PROJ_TILE
N_PROJ_TILES = N_QKV_TILES + 3 * N_CONV_TILES
TILE_V0 = N_QK_TILES
TILE_CB0 = N_QKV_TILES
TILE_CX0 = N_QKV_TILES + 2 * N_CONV_TILES
QKV_OUT_TILE = 2 * PROJ_TILE
QKV_OUT_WIDTH = N_QKV_TILES * QKV_OUT_TILE


def _qk_tile(acc, tab_ref, o_ref, first_pos, col0=0):
    tm = acc.shape[0]
    lane = lax.broadcasted_iota(jnp.int32, (tm, LANES), 1)
    pos = (first_pos + lax.broadcasted_iota(jnp.int32, (tm, LANES), 0)).astype(F32)
    first = lane < ATT_HEAD_DIM
    for c in range(acc.shape[1] // LANES):
        t = acc[:, c * LANES:(c + 1) * LANES]
        sl = slice(col0 + c * LANES, col0 + (c + 1) * LANES)
        sq = t * t
        ss0 = jnp.sum(jnp.where(first, sq, 0.0), axis=-1, keepdims=True)
        ss1 = jnp.sum(jnp.where(first, 0.0, sq), axis=-1, keepdims=True)
        r = jnp.where(first, lax.rsqrt(ss0 / ATT_HEAD_DIM + EPS), lax.rsqrt(ss1 / ATT_HEAD_DIM + EPS))
        tnorm = t * r * tab_ref[ROW_GAIN:ROW_GAIN + 1, sl]
        bias = tab_ref[ROW_SLOPE:ROW_SLOPE + 1, sl] * pos
        hi = bias.astype(BF16).astype(F32)
        rem = bias - hi
        mid = rem.astype(BF16).astype(F32)
        lo = rem - mid
        aug = (tab_ref[ROW_ONES:ROW_ONES + 1, sl] + tab_ref[ROW_HI:ROW_HI + 1, sl] * hi
               + tab_ref[ROW_MID:ROW_MID + 1, sl] * mid + tab_ref[ROW_LO:ROW_LO + 1, sl] * lo)
        comp0 = jnp.where(first, tnorm, aug)
        comp1 = jnp.where(first, pltpu.roll(tnorm, ATT_HEAD_DIM, 1), aug)
        out0 = 2 * (col0 + c * LANES)
        o_ref[:, out0:out0 + LANES] = comp0.astype(BF16)
        o_ref[:, out0 + LANES:out0 + 2 * LANES] = comp1.astype(BF16)


def _v_tile(acc, o_ref):
    tm = acc.shape[0]
    ones = jnp.ones((tm, V_GROUP - ATT_V_DIM), BF16)
    for a in range(PROJ_TILE // ATT_V_DIM):
        o_ref[:, a * V_GROUP:a * V_GROUP + ATT_V_DIM] = acc[:, a * ATT_V_DIM:(a + 1) * ATT_V_DIM].astype(BF16)
        o_ref[:, a * V_GROUP + ATT_V_DIM:(a + 1) * V_GROUP] = ones


def _inproj_kernel(x_ref, g_ref, w_ref, tab_ref, cw_ref, qkv_ref, conv_ref,
                   h_ref, carry_ref, park_ref, hold_ref, *, tm):
    i = pl.program_id(0)
    j = pl.program_id(1)

    def project(dst_ref, slot):
        dst_ref[slot] = jnp.dot(h_ref[...], w_ref[...], preferred_element_type=F32)

    def finish_conv(jc, cx_slot):
        z = hold_ref[N_CONV_TILES + jc] * park_ref[cx_slot]
        y = _causal_conv3(z, cw_ref[...], carry_ref[jc])
        conv_ref[...] = (hold_ref[jc] * y).astype(BF16)
        carry_ref[jc] = z[tm - CARRY_ROWS:tm]

    @pl.when(j == 0)
    def _():
        @pl.when(i == 0)
        def _():
            carry_ref[...] = jnp.zeros(carry_ref.shape, F32)

        h_ref[...] = _rms_rows(x_ref[...], g_ref[...]).astype(BF16)
        project(park_ref, 0)

    @pl.when(jnp.logical_and(j >= 1, j <= TILE_V0))
    def _():
        _qk_tile(park_ref[(j - 1) % 2], tab_ref, qkv_ref, i * tm)
        project(park_ref, j % 2)

    @pl.when(j == TILE_V0 + 1)
    def _():
        _v_tile(park_ref[TILE_V0 % 2], qkv_ref)
        project(park_ref, (TILE_V0 + 1) % 2)

    @pl.when(j == TILE_CB0)
    def _():
        _v_tile(park_ref[(TILE_V0 + 1) % 2], qkv_ref)
        project(hold_ref, 0)

    @pl.when(jnp.logical_and(j > TILE_CB0, j < TILE_CX0))
    def _():
        project(hold_ref, j - TILE_CB0)

    @pl.when(j == TILE_CX0)
    def _():
        project(park_ref, 0)

    @pl.when(j == TILE_CX0 + 1)
    def _():
        finish_conv(0, 0)
        project(park_ref, 1)

    @pl.when(j == TILE_CX0 + 2)
    def _():
        finish_conv(1, 1)


def _qk_table(g_q, g_k):
    n_groups = ATT_QK_WIDTH // ATT_HEAD_DIM
    slopes = jnp.asarray(2.0 ** (-8.0 * np.arange(1, ATT_HEADS + 1) / ATT_HEADS), F32)
    slope_cols = jnp.repeat(slopes, 2 * ATT_HEAD_DIM)
    lane = jnp.tile(jnp.arange(LANES), ATT_HEADS) - ATT_HEAD_DIM
    at = lambda n: (lane == n).astype(F32)
    zeros = jnp.zeros(ATT_QK_WIDTH, F32)
    both = lambda q_part, k_part: jnp.concatenate([q_part, k_part])
    rows = [None] * TABLE_ROWS
    rows[ROW_GAIN] = both(jnp.tile(g_q.astype(F32) * ATT_HEAD_DIM ** -0.5, n_groups),
                          jnp.tile(g_k.astype(F32), n_groups))
    rows[ROW_ONES] = both(at(0) + at(1) + at(2), at(3) + at(4) + at(5))
    rows[ROW_SLOPE] = both(-slope_cols, slope_cols)
    rows[ROW_HI] = both(at(3), at(0))
    rows[ROW_MID] = both(at(4), at(1))
    rows[ROW_LO] = both(at(5), at(2))
    return jnp.stack([both(zeros, zeros) if r is None else r for r in rows])


def _inproj_call(x, g_mix, w_in, table, w_conv, *, tm):
    s = x.shape[0]
    finished = lambda j, first, count: jnp.clip(j - 1 - first, 0, count - 1)
    return pl.pallas_call(
        functools.partial(_inproj_kernel, tm=tm),
        grid=(s // tm, N_PROJ_TILES + 1),
        in_specs=[
            pl.BlockSpec((tm, D_MODEL), lambda i, j: (i, 0)),
            pl.BlockSpec((1, D_MODEL), lambda i, j: (0, 0)),
            pl.BlockSpec((D_MODEL, PROJ_TILE), lambda i, j: (0, jnp.minimum(j, N_PROJ_TILES - 1))),
            pl.BlockSpec((TABLE_ROWS, PROJ_TILE), lambda i, j: (0, finished(j, 0, N_QK_TILES))),
            pl.BlockSpec((3, PROJ_TILE), lambda i, j: (0, finished(j, TILE_CX0, N_CONV_TILES))),
        ],
        out_specs=[pl.BlockSpec((tm, QKV_OUT_TILE), lambda i, j: (i, finished(j, 0, N_QKV_TILES))),
                   pl.BlockSpec((tm, PROJ_TILE),
                                lambda i, j: (i, finished(j, TILE_CX0, N_CONV_TILES)))],
        out_shape=[jax.ShapeDtypeStruct((s, QKV_OUT_WIDTH), BF16),
                   jax.ShapeDtypeStruct((s, CONV_WIDTH), BF16)],
        scratch_shapes=[pltpu.VMEM((tm, D_MODEL), BF16),
                        pltpu.VMEM((N_CONV_TILES, CARRY_ROWS, PROJ_TILE), F32),
                        pltpu.VMEM((2, tm, PROJ_TILE), F32),
                        pltpu.VMEM((2 * N_CONV_TILES, tm, PROJ_TILE), F32)],
        compiler_params=_params("arbitrary", "arbitrary"),
        name="in_proj",
    )(x, g_mix, w_in, table, w_conv)


def _attn_kernel(lam_ref, gsub_ref, q0_ref, q1_ref, k0_ref, k1_ref, v_ref, o_ref,
                 m_ref, l_ref, acc_ref, *, t, lambda_init, side_work):
    i = pl.program_id(1)
    side_work()
    m_ref[...] = jnp.full(m_ref.shape, MASK_VALUE, F32)
    l_ref[...] = jnp.zeros(l_ref.shape, F32)
    acc_ref[...] = jnp.zeros(acc_ref.shape, F32)

    q = (q0_ref[...], q1_ref[...])
    k_refs = (k0_ref, k1_ref)
    n_chunks = t // LANES

    def tile(j, masked):
        start = pl.multiple_of(j * t, t)
        v = v_ref[pl.ds(start, t), :]
        if masked:
            keep = (lax.broadcasted_iota(jnp.int32, (t, t), 1)
                    <= lax.broadcasted_iota(jnp.int32, (t, t), 0))
        for c in range(2):
            k = k_refs[c][pl.ds(start, t), :]
            s = lax.dot_general(q[c], k, NT_DIMS, preferred_element_type=F32)
            if masked:
                s = jnp.where(keep, s, MASK_VALUE)
            chunks = [s[:, a * LANES:(a + 1) * LANES] for a in range(n_chunks)]
            m_prev = m_ref[c]
            m_new = jnp.maximum(m_prev, jnp.max(functools.reduce(jnp.maximum, chunks),
                                                axis=1, keepdims=True))
            alpha = jnp.exp(m_prev - m_new)
            ps = [jnp.exp(ch - m_new) for ch in chunks]
            l_ref[c] = alpha * l_ref[c] + functools.reduce(jnp.add, ps)
            p = jnp.concatenate([x.astype(BF16) for x in ps], axis=1)
            acc_ref[c] = alpha * acc_ref[c] + jnp.dot(p, v, preferred_element_type=F32)
            m_ref[c] = m_new

    def body(j, carry):
        tile(j, False)
        return carry

    lax.fori_loop(0, i, body, 0)
    tile(i, True)

    l0 = jnp.sum(l_ref[0], axis=1, keepdims=True)
    l1 = jnp.sum(l_ref[1], axis=1, keepdims=True)
    _attn_finish(acc_ref[0], l0, acc_ref[1], l1, lam_ref, gsub_ref, o_ref, lambda_init)


def _attn_finish(acc0, l0, acc1, l1, lam_ref, gsub_ref, o_ref, lambda_init):
    lam_rows = lam_ref[...]
    lam = (jnp.exp(jnp.sum(lam_rows[0:1] * lam_rows[1:2], axis=-1, keepdims=True))
           - jnp.exp(jnp.sum(lam_rows[2:3] * lam_rows[3:4], axis=-1, keepdims=True))
           + lambda_init)
    o = acc0 / l0 - lam * (acc1 / l1)
    o_ref[...] = (_rms_rows(o, gsub_ref[...]) * (1.0 - lambda_init)).astype(BF16)


def _attn_bounded_kernel(lam_ref, gsub_ref, q0_ref, q1_ref, k0_ref, k1_ref, v_ref, o_ref, acc_ref,
                         *, t, lambda_init, wide, side_work):
    i = pl.program_id(1)
    q = (q0_ref[...], q1_ref[...])
    k_refs = (k0_ref, k1_ref)

    def tile(first, n_tiles, diagonal_last):
        start = pl.multiple_of(first * t, t)
        n_keys = n_tiles * t
        v = v_ref[pl.ds(start, n_keys), :]
        for c in range(2):
            k = k_refs[c][pl.ds(start, n_keys), :]
            s = lax.dot_general(q[c], k, NT_DIMS, preferred_element_type=F32)
            if diagonal_last:
                keep = (lax.broadcasted_iota(jnp.int32, (t, t), 1)
                        <= lax.broadcasted_iota(jnp.int32, (t, t), 0))
                parts = [jnp.where(keep, s[:, n_keys - t:], MASK_VALUE)]
                if n_tiles > 1:
                    parts.insert(0, s[:, :n_keys - t])
                s = jnp.concatenate(parts, axis=1)
            pv = jnp.dot(jnp.exp(s).astype(BF16), v, preferred_element_type=F32)
            if diagonal_last:
                acc_ref[c] = pv
            else:
                acc_ref[c] += pv

    n_wide = i // wide
    for r in range(wide):
        @pl.when(i - n_wide * wide == r)
        def _(r=r):
            side_work()
            tile(i - r, r + 1, True)

    def wide_body(j, carry):
        tile(j * wide, wide, False)
        return carry

    lax.fori_loop(0, n_wide, wide_body, 0)

    acc0 = acc_ref[0]
    acc1 = acc_ref[1]
    _attn_finish(acc0[:, :ATT_V_DIM], acc0[:, ATT_V_DIM:],
                 acc1[:, :ATT_V_DIM], acc1[:, ATT_V_DIM:],
                 lam_ref, gsub_ref, o_ref, lambda_init)


ATT_WIDE_TILES = 4


BF16_ROWS = 16
N_ATTN_INPUTS = 7


def _with_weight_casts(attn_body, n):
    def kernel(*refs):
        ins, refs = refs[:N_ATTN_INPUTS], refs[N_ATTN_INPUTS:]
        cast_in, o_ref, cast_out, scratch = refs[:n], refs[n], refs[n + 1:2 * n + 1], refs[2 * n + 1:]

        def cast_weights():
            for src, dst in zip(cast_in, cast_out):
                dst[...] = src[...].astype(BF16)

        attn_body(*ins, o_ref, *scratch, side_work=cast_weights)

    return kernel


def _attn_call(lam_rows, g_sub, qkv, *cast_arrays, cast_jobs, t, lambda_init, bounded):
    s = qkv.shape[0]
    n_q = s // t
    n_steps = ATT_HEADS * n_q
    k_off = 2 * ATT_HEADS
    v_off = 2 * ATT_QK_WIDTH // ATT_HEAD_DIM
    if bounded:
        body = functools.partial(_attn_bounded_kernel, wide=ATT_WIDE_TILES)
        v_spec = pl.BlockSpec((s, V_GROUP), lambda h, i: (0, v_off * LANES // V_GROUP + h))
        scratch = [pltpu.VMEM((2, t, V_GROUP), F32)]
    else:
        body = _attn_kernel
        v_spec = pl.BlockSpec((s, ATT_V_DIM), lambda h, i: (0, v_off + (V_GROUP // ATT_V_DIM) * h))
        scratch = [pltpu.VMEM((2, t, LANES), F32),
                   pltpu.VMEM((2, t, LANES), F32),
                   pltpu.VMEM((2, t, ATT_V_DIM), F32)]
    cast_in_specs, cast_out_specs, cast_out_shapes = [], [], []
    for arr, (width, col_block) in zip(cast_arrays, cast_jobs):
        rows = arr.shape[0]
        col_blocks = 1
        while rows % (BF16_ROWS * n_steps // col_blocks):
            col_blocks *= 2
        block = (rows * col_blocks // n_steps, width // col_blocks)
        assert block[1] % LANES == 0 and block[1] * col_blocks == width
        cast_in_specs.append(pl.BlockSpec(
            block, lambda h, i, cb=col_blocks, c0=col_block * col_blocks:
            ((h * n_q + i) // cb, c0 + (h * n_q + i) % cb)))
        cast_out_specs.append(pl.BlockSpec(
            block, lambda h, i, cb=col_blocks: ((h * n_q + i) // cb, (h * n_q + i) % cb)))
        cast_out_shapes.append(jax.ShapeDtypeStruct((rows, width), BF16))
    return pl.pallas_call(
        _with_weight_casts(functools.partial(body, t=t, lambda_init=lambda_init), len(cast_jobs)),
        grid=(ATT_HEADS, n_q),
        in_specs=[
            pl.BlockSpec((4, ATT_HEAD_DIM), lambda h, i: (0, 0)),
            pl.BlockSpec((1, ATT_V_DIM), lambda h, i: (0, 0)),
            pl.BlockSpec((t, LANES), lambda h, i: (i, 2 * h)),
            pl.BlockSpec((t, LANES), lambda h, i: (i, 2 * h + 1)),
            pl.BlockSpec((s, LANES), lambda h, i: (0, k_off + 2 * h)),
            pl.BlockSpec((s, LANES), lambda h, i: (0, k_off + 2 * h + 1)),
            v_spec,
        ] + cast_in_specs,
        out_specs=[pl.BlockSpec((t, ATT_V_DIM), lambda h, i: (i, h))] + cast_out_specs,
        out_shape=[jax.ShapeDtypeStruct((s, ATT_WIDTH), BF16)] + cast_out_shapes,
        scratch_shapes=scratch,
        compiler_params=_params("arbitrary", "arbitrary"),
        name="diff_attn_bounded" if bounded else "diff_attn",
    )(lam_rows, g_sub, qkv, qkv, qkv, qkv, qkv, *cast_arrays)


SCORE_BOUND_LIMIT = 40.0


def _attention(lam_rows, g_sub, g_q, g_k, qkv, cast_arrays, cast_jobs, *, t, lambda_init):
    bound = (1.01 * ATT_HEAD_DIM ** 0.5
             * jnp.max(jnp.abs(g_q.astype(F32))) * jnp.max(jnp.abs(g_k.astype(F32))))
    call = functools.partial(_attn_call, cast_jobs=cast_jobs, t=t, lambda_init=lambda_init)
    return lax.cond(bound <= SCORE_BOUND_LIMIT,
                    functools.partial(call, bounded=True),
                    functools.partial(call, bounded=False),
                    lam_rows, g_sub, qkv, *cast_arrays)


def _merge_kernel(x_ref, g_ref, ya_ref, yc_ref, wga_ref, wgc_ref, bg_ref, wa_ref, wc_ref, wo_ref,
                  gffn_ref, x1_ref, h2_ref):
    xf = x_ref[...]
    h = _rms_rows(xf, g_ref[...]).astype(BF16)
    bg = bg_ref[...]
    gate_a = jax.nn.sigmoid(jnp.dot(h, wga_ref[...], preferred_element_type=F32) + bg[0:1])
    merged = gate_a * jnp.dot(ya_ref[...], wa_ref[...], preferred_element_type=F32)
    gate_c = jax.nn.sigmoid(jnp.dot(h, wgc_ref[...], preferred_element_type=F32) + bg[1:2])
    merged = merged + gate_c * jnp.dot(yc_ref[...], wc_ref[...], preferred_element_type=F32)
    x1 = xf + jnp.dot(merged.astype(BF16), wo_ref[...], preferred_element_type=F32)
    x1_ref[...] = x1
    h2_ref[...] = _rms_rows(x1, gffn_ref[...]).astype(BF16)


def _merge_call(x, g_mix, y_att, y_conv, w_ga, w_gc, b_gates, w_att_out, w_conv_out, w_o, g_ffn,
                *, tm):
    s = x.shape[0]
    const = lambda i: (0, 0)
    return pl.pallas_call(
        _merge_kernel,
        grid=(s // tm,),
        in_specs=[
            pl.BlockSpec((tm, D_MODEL), lambda i: (i, 0)),
            _resident((1, D_MODEL), const),
            pl.BlockSpec((tm, ATT_WIDTH), lambda i: (i, 0)),
            pl.BlockSpec((tm, CONV_WIDTH), lambda i: (i, 0)),
            _resident((D_MODEL, D_MODEL), const),
            _resident((D_MODEL, D_MODEL), const),
            _resident((2, D_MODEL), const),
            _resident((ATT_WIDTH, D_MODEL), const),
            _resident((CONV_WIDTH, D_MODEL), const),
            _resident((D_MODEL, D_MODEL), const),
            _resident((1, D_MODEL), const),
        ],
        out_specs=[pl.BlockSpec((tm, D_MODEL), lambda i: (i, 0)),
                   pl.BlockSpec((tm, D_MODEL), lambda i: (i, 0))],
        out_shape=[jax.ShapeDtypeStruct((s, D_MODEL), F32),
                   jax.ShapeDtypeStruct((s, D_MODEL), BF16)],
        compiler_params=_params("arbitrary"),
        name="merge",
    )(x, g_mix, y_att, y_conv, w_ga, w_gc, b_gates, w_att_out, w_conv_out, w_o, g_ffn)


FFN_SLICE = 256


def _ffn_kernel(h_ref, wg_ref, wv_ref, cwg_ref, cwv_ref, wd_ref, o_ref,
                carry_ref, pre_ref, act_ref, *, tm):
    i = pl.program_id(0)
    f = pl.program_id(1)

    @pl.when(i == 0)
    def _():
        carry_ref[f] = jnp.zeros(carry_ref.shape[1:], F32)

    @pl.when(f == 0)
    def _():
        o_ref[...] = jnp.zeros(o_ref.shape, F32)

    h = h_ref[...]
    n_slices = wg_ref.shape[1] // FFN_SLICE
    slices = [slice(a * FFN_SLICE, (a + 1) * FFN_SLICE) for a in range(n_slices)]
    for a, sl in enumerate(slices):
        for b, (w_ref, cw_ref) in enumerate(((wg_ref, cwg_ref), (wv_ref, cwv_ref))):
            pre_ref[b, a, 0:CARRY_ROWS] = carry_ref[f, b, :, sl]
            pre_ref[b, a, CARRY_ROWS:CARRY_ROWS + tm] = jnp.dot(h, w_ref[:, sl],
                                                                preferred_element_type=F32)
            carry_ref[f, b, :, sl] = pre_ref[b, a, tm:tm + CARRY_ROWS]
    for a, sl in enumerate(slices):
        u = []
        for b, cw_ref in enumerate((cwg_ref, cwv_ref)):
            w = cw_ref[:, sl]
            u.append(w[0:1] * pre_ref[b, a, CARRY_ROWS - 2:CARRY_ROWS - 2 + tm]
                     + w[1:2] * pre_ref[b, a, CARRY_ROWS - 1:CARRY_ROWS - 1 + tm]
                     + w[2:3] * pre_ref[b, a, CARRY_ROWS:CARRY_ROWS + tm])
        act_ref[a] = (u[0] * jax.nn.sigmoid(u[0]) * u[1]).astype(BF16)
    for a, sl in enumerate(slices):
        for n in range(o_ref.shape[1] // FFN_SLICE):
            cs = slice(n * FFN_SLICE, (n + 1) * FFN_SLICE)
            o_ref[:, cs] += jnp.dot(act_ref[a], wd_ref[sl, cs], preferred_element_type=F32)


def _ffn_call(h2, w_up, w_ffn_conv, w_down, *, tm, tf):
    s = h2.shape[0]
    n_f = D_FF // tf
    return pl.pallas_call(
        functools.partial(_ffn_kernel, tm=tm),
        grid=(s // tm, n_f),
        in_specs=[
            pl.BlockSpec((tm, D_MODEL), lambda i, f: (i, 0)),
            pl.BlockSpec((D_MODEL, tf), lambda i, f: (0, f)),
            pl.BlockSpec((D_MODEL, tf), lambda i, f: (0, n_f + f)),
            pl.BlockSpec((3, tf), lambda i, f: (0, f)),
            pl.BlockSpec((3, tf), lambda i, f: (0, n_f + f)),
            pl.BlockSpec((tf, D_MODEL), lambda i, f: (f, 0)),
        ],
        out_specs=pl.BlockSpec((tm, D_MODEL), lambda i, f: (i, 0)),
        out_shape=jax.ShapeDtypeStruct((s, D_MODEL), F32),
        scratch_shapes=[pltpu.VMEM((n_f, 2, CARRY_ROWS, tf), F32),
                        pltpu.VMEM((2, tf // FFN_SLICE, CARRY_ROWS + tm, FFN_SLICE), F32),
                        pltpu.VMEM((tf // FFN_SLICE, tm, FFN_SLICE), BF16)],
        compiler_params=_params("arbitrary", "arbitrary"),
        name="conv_ffn",
    )(h2, w_up, w_up, w_ffn_conv, w_ffn_conv, w_down)


def _ple_kernel(x_ref, y_ref, p_ref, wple_ref, gple_ref, gpg_ref, wpg_ref, bpg_ref, o_ref):
    xf = x_ref[...] + y_ref[...]
    pe = _rms_rows(jnp.dot(p_ref[...].astype(BF16), wple_ref[...], preferred_element_type=F32),
                   gple_ref[...])
    hn = _rms_rows(xf, gpg_ref[...]).astype(BF16)
    pg = jax.nn.sigmoid(jnp.dot(hn, wpg_ref[...], preferred_element_type=F32) + bpg_ref[...])
    o_ref[...] = xf + pg * pe


def _ple_call(x1, y_ffn, p, w_ple, g_ple, g_pg, w_pg, b_pg, *, tm):
    s = x1.shape[0]
    const = lambda i: (0, 0)
    return pl.pallas_call(
        _ple_kernel,
        grid=(s // tm,),
        in_specs=[
            pl.BlockSpec((tm, D_MODEL), lambda i: (i, 0)),
            pl.BlockSpec((tm, D_MODEL), lambda i: (i, 0)),
            pl.BlockSpec((tm, PLE_DIM), lambda i: (i, 0)),
            _resident((PLE_DIM, D_MODEL), const),
            _resident((1, D_MODEL), const),
            _resident((1, D_MODEL), const),
            _resident((D_MODEL, D_MODEL), const),
            _resident((1, D_MODEL), const),
        ],
        out_specs=pl.BlockSpec((tm, D_MODEL), lambda i: (i, 0)),
        out_shape=jax.ShapeDtypeStruct((s, D_MODEL), F32),
        compiler_params=_params("arbitrary"),
        name="ple_gate",
    )(x1, y_ffn, p, w_ple, g_ple, g_pg, w_pg, b_pg)


def _layer(depth_index, x, p, g_mix, w_in, b_gates, g_q, g_k, lam_rows, g_sub, w_conv_mix, w_att_out,
           w_conv_out, w_o, g_ffn, w_up, w_ffn_conv, w_down, w_ple, g_ple, g_pg, w_pg, b_pg):
    lambda_init = 0.8 - 0.6 * math.exp(-0.3 * depth_index)
    row = lambda a: a.reshape(1, -1).astype(F32)

    g_mix = row(g_mix)
    w_proj = w_in[:, :OFF_GA].astype(BF16)
    qkv, y_conv = _inproj_call(x, g_mix, w_proj, _qk_table(g_q, g_k), w_conv_mix.astype(F32), tm=1024)
    later = [w_in, w_in, w_att_out, w_conv_out, w_o, w_up, w_down, w_ple, w_pg]
    jobs = ([(D_MODEL, OFF_GA // D_MODEL), (D_MODEL, OFF_GC // D_MODEL)]
            + [(w.shape[1], 0) for w in later[2:]])
    y_att, w_ga, w_gc, w_att_out, w_conv_out, w_o, w_up, w_down, w_ple, w_pg = _attention(
        lam_rows, row(g_sub), g_q, g_k, qkv, [w.astype(F32) for w in later], jobs,
        t=1024, lambda_init=lambda_init)
    x1, h2 = _merge_call(x, g_mix, y_att, y_conv, w_ga, w_gc, b_gates.astype(F32),
                         w_att_out, w_conv_out, w_o, row(g_ffn), tm=256)
    y_ffn = _ffn_call(h2, w_up, w_ffn_conv.astype(F32), w_down, tm=1024, tf=512)
    return _ple_call(x1, y_ffn, p, w_ple, row(g_ple), row(g_pg), w_pg, row(b_pg), tm=512)


def kernel(x, p, g_mix, w_in, b_gates, g_q, g_k, lam_q1, lam_k1, lam_q2, lam_k2, g_sub, w_conv_mix,
           w_att_out, w_conv_out, w_o, g_ffn, w_up, w_ffn_conv, w_down, w_ple, g_ple, g_pg, w_pg,
           b_pg):
    batch, seq, d_model = x.shape
    depth = p.shape[0]
    outs = []
    for b in range(batch):
        xb = x.reshape(seq, d_model) if batch == 1 else x[b]
        for i in range(depth):
            lam_rows = jnp.stack([lam_q1[i], lam_k1[i], lam_q2[i], lam_k2[i]]).astype(F32)
            pb = p.reshape(seq, PLE_DIM) if batch == 1 and depth == 1 else p[i, b]
            xb = _layer(i, xb, pb, g_mix[i], w_in[i], b_gates[i], g_q[i], g_k[i], lam_rows,
                        g_sub[i], w_conv_mix[i], w_att_out[i], w_conv_out[i], w_o[i], g_ffn[i],
                        w_up[i], w_ffn_conv[i], w_down[i], w_ple[i], g_ple[i], g_pg[i], w_pg[i],
                        b_pg[i])
        outs.append(xb)
    return outs[0].reshape(x.shape) if batch == 1 else jnp.stack(outs)
```

```python
import functools
import math

import jax
import jax.numpy as jnp
import numpy as np
from jax import lax
from jax.experimental import pallas as pl
from jax.experimental.pallas import tpu as pltpu

F32 = jnp.float32
BF16 = jnp.bfloat16

D_MODEL = 2048
ATT_HEADS = 8
ATT_HEAD_DIM = 64
ATT_V_DIM = 2 * ATT_HEAD_DIM
ATT_QK_WIDTH = ATT_HEADS * 2 * ATT_HEAD_DIM
ATT_WIDTH = ATT_HEADS * ATT_V_DIM
CONV_WIDTH = 1024
D_FF = 5632
PLE_DIM = 256
EPS = 1e-6

LANES = 128
CARRY_ROWS = 8
MASK_VALUE = -1e30
VMEM_LIMIT = 56 * 1024 * 1024

OFF_GA = 2 * ATT_QK_WIDTH + ATT_WIDTH + 3 * CONV_WIDTH
OFF_GC = OFF_GA + D_MODEL

NT_DIMS = (((1,), (1,)), ((), ()))


def _rms_rows(xf, g):
    ms = jnp.mean(xf * xf, axis=-1, keepdims=True)
    return xf * lax.rsqrt(ms + EPS) * g


def _params(*sem):
    return pltpu.CompilerParams(dimension_semantics=sem, vmem_limit_bytes=VMEM_LIMIT)


def _resident(shape, index_map):
    return pl.BlockSpec(shape, index_map, pipeline_mode=pl.Buffered(1))


def _shift_rows(u, prev, rows):
    u1 = jnp.where(rows == 0, prev[CARRY_ROWS - 1:CARRY_ROWS], pltpu.roll(u, 1, 0))
    u2 = jnp.where(rows == 0, prev[CARRY_ROWS - 2:CARRY_ROWS - 1],
                   jnp.where(rows == 1, prev[CARRY_ROWS - 1:CARRY_ROWS], pltpu.roll(u, 2, 0)))
    return u1, u2


def _causal_conv3(u, w, prev):
    rows = lax.broadcasted_iota(jnp.int32, u.shape, 0)
    u1, u2 = _shift_rows(u, prev, rows)
    return w[0:1] * u2 + w[1:2] * u1 + w[2:3] * u


ROW_GAIN, ROW_ONES, ROW_SLOPE, ROW_HI, ROW_MID, ROW_LO, TABLE_ROWS = 0, 1, 2, 3, 4, 5, 8
V_GROUP = 2 * LANES
PROJ_TILE = 512
N_QK_TILES = 2 * ATT_QK_WIDTH // PROJ_TILE
N_QKV_TILES = N_QK_TILES + ATT_WIDTH // PROJ_TILE
N_CONV_TILES = CONV_WIDTH // PROJ_TILE
N_PROJ_TILES = N_QKV_TILES + 3 * N_CONV_TILES
TILE_V0 = N_QK_TILES
TILE_CB0 = N_QKV_TILES
TILE_CX0 = N_QKV_TILES + 2 * N_CONV_TILES
QKV_OUT_TILE = 2 * PROJ_TILE
QKV_OUT_WIDTH = N_QKV_TILES * QKV_OUT_TILE


def _qk_tile(acc, tab_ref, o_ref, first_pos):
    tm = acc.shape[0]
    lane = lax.broadcasted_iota(jnp.int32, (tm, LANES), 1)
    pos = (first_pos + lax.broadcasted_iota(jnp.int32, (tm, LANES), 0)).astype(F32)
    first = lane < ATT_HEAD_DIM
    for c in range(acc.shape[1] // LANES):
        t = acc[:, c * LANES:(c + 1) * LANES]
        sl = slice(c * LANES, (c + 1) * LANES)
        sq = t * t
        ss0 = jnp.sum(jnp.where(first, sq, 0.0), axis=-1, keepdims=True)
        ss1 = jnp.sum(jnp.where(first, 0.0, sq), axis=-1, keepdims=True)
        r = jnp.where(first, lax.rsqrt(ss0 / ATT_HEAD_DIM + EPS), lax.rsqrt(ss1 / ATT_HEAD_DIM + EPS))
        tnorm = t * r * tab_ref[ROW_GAIN:ROW_GAIN + 1, sl]
        bias = tab_ref[ROW_SLOPE:ROW_SLOPE + 1, sl] * pos
        hi = bias.astype(BF16).astype(F32)
        rem = bias - hi
        mid = rem.astype(BF16).astype(F32)
        lo = rem - mid
        aug = (tab_ref[ROW_ONES:ROW_ONES + 1, sl] + tab_ref[ROW_HI:ROW_HI + 1, sl] * hi
               + tab_ref[ROW_MID:ROW_MID + 1, sl] * mid + tab_ref[ROW_LO:ROW_LO + 1, sl] * lo)
        comp0 = jnp.where(first, tnorm, aug)
        comp1 = jnp.where(first, pltpu.roll(tnorm, ATT_HEAD_DIM, 1), aug)
        out0 = 2 * c * LANES
        o_ref[:, out0:out0 + LANES] = comp0.astype(BF16)
        o_ref[:, out0 + LANES:out0 + 2 * LANES] = comp1.astype(BF16)


def _v_tile(acc, o_ref):
    tm = acc.shape[0]
    ones = jnp.ones((tm, V_GROUP - ATT_V_DIM), BF16)
    for a in range(PROJ_TILE // ATT_V_DIM):
        o_ref[:, a * V_GROUP:a * V_GROUP + ATT_V_DIM] = acc[:, a * ATT_V_DIM:(a + 1) * ATT_V_DIM].astype(BF16)
        o_ref[:, a * V_GROUP + ATT_V_DIM:(a + 1) * V_GROUP] = ones


def _inproj_kernel(x_ref, g_ref, w_ref, tab_ref, cw_ref, qkv_ref, conv_ref,
                   h_ref, carry_ref, park_ref, hold_ref, *, tm):
    i = pl.program_id(0)
    j = pl.program_id(1)

    def project(dst_ref, slot):
        dst_ref[slot] = jnp.dot(h_ref[...], w_ref[...].astype(BF16), preferred_element_type=F32)

    def finish_conv(jc, cx_slot):
        z = hold_ref[N_CONV_TILES + jc] * park_ref[cx_slot]
        y = _causal_conv3(z, cw_ref[...], carry_ref[jc])
        conv_ref[...] = (hold_ref[jc] * y).astype(BF16)
        carry_ref[jc] = z[tm - CARRY_ROWS:tm]

    @pl.when(j == 0)
    def _():
        @pl.when(i == 0)
        def _():
            carry_ref[...] = jnp.zeros(carry_ref.shape, F32)

        h_ref[...] = _rms_rows(x_ref[...], g_ref[...]).astype(BF16)
        project(park_ref, 0)

    @pl.when(jnp.logical_and(j >= 1, j <= TILE_V0))
    def _():
        _qk_tile(park_ref[(j - 1) % 2], tab_ref, qkv_ref, i * tm)
        project(park_ref, j % 2)

    @pl.when(j == TILE_V0 + 1)
    def _():
        _v_tile(park_ref[TILE_V0 % 2], qkv_ref)
        project(park_ref, (TILE_V0 + 1) % 2)

    @pl.when(j == TILE_CB0)
    def _():
        _v_tile(park_ref[(TILE_V0 + 1) % 2], qkv_ref)
        project(hold_ref, 0)

    @pl.when(jnp.logical_and(j > TILE_CB0, j < TILE_CX0))
    def _():
        project(hold_ref, j - TILE_CB0)

    @pl.when(j == TILE_CX0)
    def _():
        project(park_ref, 0)

    @pl.when(j == TILE_CX0 + 1)
    def _():
        finish_conv(0, 0)
        project(park_ref, 1)

    @pl.when(j == TILE_CX0 + 2)
    def _():
        finish_conv(1, 1)


def _qk_table(g_q, g_k):
    n_groups = ATT_QK_WIDTH // ATT_HEAD_DIM
    slopes = jnp.asarray(2.0 ** (-8.0 * np.arange(1, ATT_HEADS + 1) / ATT_HEADS), F32)
    slope_cols = jnp.repeat(slopes, 2 * ATT_HEAD_DIM)
    lane = jnp.tile(jnp.arange(LANES), ATT_HEADS) - ATT_HEAD_DIM
    at = lambda n: (lane == n).astype(F32)
    zeros = jnp.zeros(ATT_QK_WIDTH, F32)
    both = lambda q_part, k_part: jnp.concatenate([q_part, k_part])
    rows = [None] * TABLE_ROWS
    rows[ROW_GAIN] = both(jnp.tile(g_q.astype(F32) * ATT_HEAD_DIM ** -0.5, n_groups),
                          jnp.tile(g_k.astype(F32), n_groups))
    rows[ROW_ONES] = both(at(0) + at(1) + at(2), at(3) + at(4) + at(5))
    rows[ROW_SLOPE] = both(-slope_cols, slope_cols)
    rows[ROW_HI] = both(at(3), at(0))
    rows[ROW_MID] = both(at(4), at(1))
    rows[ROW_LO] = both(at(5), at(2))
    return jnp.stack([both(zeros, zeros) if r is None else r for r in rows])


def _inproj_call(x, g_mix, w_in, table, w_conv, *, tm):
    s = x.shape[0]
    finished = lambda j, first, count: jnp.clip(j - 1 - first, 0, count - 1)
    return pl.pallas_call(
        functools.partial(_inproj_kernel, tm=tm),
        grid=(s // tm, N_PROJ_TILES + 1),
        in_specs=[
            pl.BlockSpec((tm, D_MODEL), lambda i, j: (i, 0)),
            pl.BlockSpec((1, D_MODEL), lambda i, j: (0, 0)),
            pl.BlockSpec((D_MODEL, PROJ_TILE), lambda i, j: (0, jnp.minimum(j, N_PROJ_TILES - 1))),
            pl.BlockSpec((TABLE_ROWS, PROJ_TILE), lambda i, j: (0, finished(j, 0, N_QK_TILES))),
            pl.BlockSpec((3, PROJ_TILE), lambda i, j: (0, finished(j, TILE_CX0, N_CONV_TILES))),
        ],
        out_specs=[pl.BlockSpec((tm, QKV_OUT_TILE), lambda i, j: (i, finished(j, 0, N_QKV_TILES))),
                   pl.BlockSpec((tm, PROJ_TILE),
                                lambda i, j: (i, finished(j, TILE_CX0, N_CONV_TILES)))],
        out_shape=[jax.ShapeDtypeStruct((s, QKV_OUT_WIDTH), BF16),
                   jax.ShapeDtypeStruct((s, CONV_WIDTH), BF16)],
        scratch_shapes=[pltpu.VMEM((tm, D_MODEL), BF16),
                        pltpu.VMEM((N_CONV_TILES, CARRY_ROWS, PROJ_TILE), F32),
                        pltpu.VMEM((2, tm, PROJ_TILE), F32),
                        pltpu.VMEM((2 * N_CONV_TILES, tm, PROJ_TILE), F32)],
        compiler_params=_params("arbitrary", "arbitrary"),
        name="in_proj",
    )(x, g_mix, w_in, table, w_conv)


def _attn_kernel(lam_ref, gsub_ref, q0_ref, q1_ref, k0_ref, k1_ref, v_ref, o_ref,
                 m_ref, l_ref, acc_ref, *, t, lambda_init, side_work):
    i = pl.program_id(1)
    side_work()
    m_ref[...] = jnp.full(m_ref.shape, MASK_VALUE, F32)
    l_ref[...] = jnp.zeros(l_ref.shape, F32)
    acc_ref[...] = jnp.zeros(acc_ref.shape, F32)

    q = (q0_ref[...], q1_ref[...])
    k_refs = (k0_ref, k1_ref)
    n_chunks = t // LANES

    def tile(j, masked):
        start = pl.multiple_of(j * t, t)
        v = v_ref[pl.ds(start, t), :]
        if masked:
            keep = (lax.broadcasted_iota(jnp.int32, (t, t), 1)
                    <= lax.broadcasted_iota(jnp.int32, (t, t), 0))
        for c in range(2):
            k = k_refs[c][pl.ds(start, t), :]
            s = lax.dot_general(q[c], k, NT_DIMS, preferred_element_type=F32)
            if masked:
                s = jnp.where(keep, s, MASK_VALUE)
            chunks = [s[:, a * LANES:(a + 1) * LANES] for a in range(n_chunks)]
            m_prev = m_ref[c]
            m_new = jnp.maximum(m_prev, jnp.max(functools.reduce(jnp.maximum, chunks),
                                                axis=1, keepdims=True))
            alpha = jnp.exp(m_prev - m_new)
            ps = [jnp.exp(ch - m_new) for ch in chunks]
            l_ref[c] = alpha * l_ref[c] + functools.reduce(jnp.add, ps)
            p = jnp.concatenate([x.astype(BF16) for x in ps], axis=1)
            acc_ref[c] = alpha * acc_ref[c] + jnp.dot(p, v, preferred_element_type=F32)
            m_ref[c] = m_new

    def body(j, carry):
        tile(j, False)
        return carry

    lax.fori_loop(0, i, body, 0)
    tile(i, True)

    l0 = jnp.sum(l_ref[0], axis=1, keepdims=True)
    l1 = jnp.sum(l_ref[1], axis=1, keepdims=True)
    _attn_finish(acc_ref[0], l0, acc_ref[1], l1, lam_ref, gsub_ref, o_ref, lambda_init)


def _attn_finish(acc0, l0, acc1, l1, lam_ref, gsub_ref, o_ref, lambda_init):
    lam_rows = lam_ref[...]
    lam = (jnp.exp(jnp.sum(lam_rows[0:1] * lam_rows[1:2], axis=-1, keepdims=True))
           - jnp.exp(jnp.sum(lam_rows[2:3] * lam_rows[3:4], axis=-1, keepdims=True))
           + lambda_init)
    o = acc0 / l0 - lam * (acc1 / l1)
    o_ref[...] = (_rms_rows(o, gsub_ref[...]) * (1.0 - lambda_init)).astype(BF16)


def _attn_bounded_kernel(lam_ref, gsub_ref, q0_ref, q1_ref, k0_ref, k1_ref, v_ref, o_ref, acc_ref,
                         *, t, lambda_init, wide, side_work):
    i = pl.program_id(1)
    q = (q0_ref[...], q1_ref[...])
    k_refs = (k0_ref, k1_ref)

    def tile(first, n_tiles, diagonal_last):
        start = pl.multiple_of(first * t, t)
        n_keys = n_tiles * t
        v = v_ref[pl.ds(start, n_keys), :]
        for c in range(2):
            k = k_refs[c][pl.ds(start, n_keys), :]
            s = lax.dot_general(q[c], k, NT_DIMS, preferred_element_type=F32)
            if diagonal_last:
                keep = (lax.broadcasted_iota(jnp.int32, (t, t), 1)
                        <= lax.broadcasted_iota(jnp.int32, (t, t), 0))
                parts = [jnp.where(keep, s[:, n_keys - t:], MASK_VALUE)]
                if n_tiles > 1:
                    parts.insert(0, s[:, :n_keys - t])
                s = jnp.concatenate(parts, axis=1)
            pv = jnp.dot(jnp.exp(s).astype(BF16), v, preferred_element_type=F32)
            if diagonal_last:
                acc_ref[c] = pv
            else:
                acc_ref[c] += pv

    n_wide = i // wide
    for r in range(wide):
        @pl.when(i - n_wide * wide == r)
        def _(r=r):
            side_work()
            tile(i - r, r + 1, True)

    def wide_body(j, carry):
        tile(j * wide, wide, False)
        return carry

    lax.fori_loop(0, n_wide, wide_body, 0)

    acc0 = acc_ref[0]
    acc1 = acc_ref[1]
    _attn_finish(acc0[:, :ATT_V_DIM], acc0[:, ATT_V_DIM:],
                 acc1[:, :ATT_V_DIM], acc1[:, ATT_V_DIM:],
                 lam_ref, gsub_ref, o_ref, lambda_init)


ATT_WIDE_TILES = 4


BF16_ROWS = 16
N_ATTN_INPUTS = 7


def _with_weight_casts(attn_body, n):
    def kernel(*refs):
        ins, refs = refs[:N_ATTN_INPUTS], refs[N_ATTN_INPUTS:]
        cast_in, o_ref, cast_out, scratch = refs[:n], refs[n], refs[n + 1:2 * n + 1], refs[2 * n + 1:]

        def cast_weights():
            for src, dst in zip(cast_in, cast_out):
                dst[...] = src[...].astype(BF16)

        attn_body(*ins, o_ref, *scratch, side_work=cast_weights)

    return kernel


def _attn_call(lam_rows, g_sub, qkv, *cast_arrays, cast_jobs, t, lambda_init, bounded):
    s = qkv.shape[0]
    n_q = s // t
    n_steps = ATT_HEADS * n_q
    k_off = 2 * ATT_HEADS
    v_off = 2 * ATT_QK_WIDTH // ATT_HEAD_DIM
    if bounded:
        body = functools.partial(_attn_bounded_kernel, wide=ATT_WIDE_TILES)
        v_spec = pl.BlockSpec((s, V_GROUP), lambda h, i: (0, v_off * LANES // V_GROUP + h))
        scratch = [pltpu.VMEM((2, t, V_GROUP), F32)]
    else:
        body = _attn_kernel
        v_spec = pl.BlockSpec((s, ATT_V_DIM), lambda h, i: (0, v_off + (V_GROUP // ATT_V_DIM) * h))
        scratch = [pltpu.VMEM((2, t, LANES), F32),
                   pltpu.VMEM((2, t, LANES), F32),
                   pltpu.VMEM((2, t, ATT_V_DIM), F32)]
    cast_in_specs, cast_out_specs, cast_out_shapes = [], [], []
    for arr, (width, col_block) in zip(cast_arrays, cast_jobs):
        rows = arr.shape[0]
        col_blocks = 1
        while rows % (BF16_ROWS * n_steps // col_blocks):
            col_blocks *= 2
        block = (rows * col_blocks // n_steps, width // col_blocks)
        assert block[1] % LANES == 0 and block[1] * col_blocks == width
        cast_in_specs.append(pl.BlockSpec(
            block, lambda h, i, cb=col_blocks, c0=col_block * col_blocks:
            ((h * n_q + i) // cb, c0 + (h * n_q + i) % cb)))
        cast_out_specs.append(pl.BlockSpec(
            block, lambda h, i, cb=col_blocks: ((h * n_q + i) // cb, (h * n_q + i) % cb)))
        cast_out_shapes.append(jax.ShapeDtypeStruct((rows, width), BF16))
    return pl.pallas_call(
        _with_weight_casts(functools.partial(body, t=t, lambda_init=lambda_init), len(cast_jobs)),
        grid=(ATT_HEADS, n_q),
        in_specs=[
            pl.BlockSpec((4, ATT_HEAD_DIM), lambda h, i: (0, 0)),
            pl.BlockSpec((1, ATT_V_DIM), lambda h, i: (0, 0)),
            pl.BlockSpec((t, LANES), lambda h, i: (i, 2 * h)),
            pl.BlockSpec((t, LANES), lambda h, i: (i, 2 * h + 1)),
            pl.BlockSpec((s, LANES), lambda h, i: (0, k_off + 2 * h)),
            pl.BlockSpec((s, LANES), lambda h, i: (0, k_off + 2 * h + 1)),
            v_spec,
        ] + cast_in_specs,
        out_specs=[pl.BlockSpec((t, ATT_V_DIM), lambda h, i: (i, h))] + cast_out_specs,
        out_shape=[jax.ShapeDtypeStruct((s, ATT_WIDTH), BF16)] + cast_out_shapes,
        scratch_shapes=scratch,
        compiler_params=_params("arbitrary", "arbitrary"),
        name="diff_attn_bounded" if bounded else "diff_attn",
    )(lam_rows, g_sub, qkv, qkv, qkv, qkv, qkv, *cast_arrays)


SCORE_BOUND_LIMIT = 40.0


def _attention(lam_rows, g_sub, g_q, g_k, qkv, cast_arrays, cast_jobs, *, t, lambda_init):
    bound = (1.01 * ATT_HEAD_DIM ** 0.5
             * jnp.max(jnp.abs(g_q.astype(F32))) * jnp.max(jnp.abs(g_k.astype(F32))))
    call = functools.partial(_attn_call, cast_jobs=cast_jobs, t=t, lambda_init=lambda_init)
    return lax.cond(bound <= SCORE_BOUND_LIMIT,
                    functools.partial(call, bounded=True),
                    functools.partial(call, bounded=False),
                    lam_rows, g_sub, qkv, *cast_arrays)


def _merge_kernel(x_ref, g_ref, ya_ref, yc_ref, wga_ref, wgc_ref, bg_ref, wa_ref, wc_ref, wo_ref,
                  gffn_ref, x1_ref, h2_ref):
    xf = x_ref[...]
    h = _rms_rows(xf, g_ref[...]).astype(BF16)
    bg = bg_ref[...]
    gate_a = jax.nn.sigmoid(jnp.dot(h, wga_ref[...], preferred_element_type=F32) + bg[0:1])
    merged = gate_a * jnp.dot(ya_ref[...], wa_ref[...], preferred_element_type=F32)
    gate_c = jax.nn.sigmoid(jnp.dot(h, wgc_ref[...], preferred_element_type=F32) + bg[1:2])
    merged = merged + gate_c * jnp.dot(yc_ref[...], wc_ref[...], preferred_element_type=F32)
    x1 = xf + jnp.dot(merged.astype(BF16), wo_ref[...], preferred_element_type=F32)
    x1_ref[...] = x1
    h2_ref[...] = _rms_rows(x1, gffn_ref[...]).astype(BF16)


def _merge_call(x, g_mix, y_att, y_conv, w_ga, w_gc, b_gates, w_att_out, w_conv_out, w_o, g_ffn,
                *, tm):
    s = x.shape[0]
    const = lambda i: (0, 0)
    return pl.pallas_call(
        _merge_kernel,
        grid=(s // tm,),
        in_specs=[
            pl.BlockSpec((tm, D_MODEL), lambda i: (i, 0)),
            _resident((1, D_MODEL), const),
            pl.BlockSpec((tm, ATT_WIDTH), lambda i: (i, 0)),
            pl.BlockSpec((tm, CONV_WIDTH), lambda i: (i, 0)),
            _resident((D_MODEL, D_MODEL), const),
            _resident((D_MODEL, D_MODEL), const),
            _resident((2, D_MODEL), const),
            _resident((ATT_WIDTH, D_MODEL), const),
            _resident((CONV_WIDTH, D_MODEL), const),
            _resident((D_MODEL, D_MODEL), const),
            _resident((1, D_MODEL), const),
        ],
        out_specs=[pl.BlockSpec((tm, D_MODEL), lambda i: (i, 0)),
                   pl.BlockSpec((tm, D_MODEL), lambda i: (i, 0))],
        out_shape=[jax.ShapeDtypeStruct((s, D_MODEL), F32),
                   jax.ShapeDtypeStruct((s, D_MODEL), BF16)],
        compiler_params=_params("arbitrary"),
        name="merge",
    )(x, g_mix, y_att, y_conv, w_ga, w_gc, b_gates, w_att_out, w_conv_out, w_o, g_ffn)


FFN_SLICE = 256


def _ffn_kernel(h_ref, wg_ref, wv_ref, cwg_ref, cwv_ref, wd_ref, o_ref,
                carry_ref, pre_ref, act_ref, *, tm):
    i = pl.program_id(0)
    f = pl.program_id(1)

    @pl.when(i == 0)
    def _():
        carry_ref[f] = jnp.zeros(carry_ref.shape[1:], F32)

    @pl.when(f == 0)
    def _():
        o_ref[...] = jnp.zeros(o_ref.shape, F32)

    h = h_ref[...]
    n_slices = wg_ref.shape[1] // FFN_SLICE
    slices = [slice(a * FFN_SLICE, (a + 1) * FFN_SLICE) for a in range(n_slices)]
    for a, sl in enumerate(slices):
        for b, (w_ref, cw_ref) in enumerate(((wg_ref, cwg_ref), (wv_ref, cwv_ref))):
            pre_ref[b, a, 0:CARRY_ROWS] = carry_ref[f, b, :, sl]
            pre_ref[b, a, CARRY_ROWS:CARRY_ROWS + tm] = jnp.dot(h, w_ref[:, sl],
                                                                preferred_element_type=F32)
            carry_ref[f, b, :, sl] = pre_ref[b, a, tm:tm + CARRY_ROWS]
    for a, sl in enumerate(slices):
        u = []
        for b, cw_ref in enumerate((cwg_ref, cwv_ref)):
            w = cw_ref[:, sl]
            u.append(w[0:1] * pre_ref[b, a, CARRY_ROWS - 2:CARRY_ROWS - 2 + tm]
                     + w[1:2] * pre_ref[b, a, CARRY_ROWS - 1:CARRY_ROWS - 1 + tm]
                     + w[2:3] * pre_ref[b, a, CARRY_ROWS:CARRY_ROWS + tm])
        act_ref[a] = (u[0] * jax.nn.sigmoid(u[0]) * u[1]).astype(BF16)
    for a, sl in enumerate(slices):
        for n in range(o_ref.shape[1] // FFN_SLICE):
            cs = slice(n * FFN_SLICE, (n + 1) * FFN_SLICE)
            o_ref[:, cs] += jnp.dot(act_ref[a], wd_ref[sl, cs], preferred_element_type=F32)


def _ffn_call(h2, w_up, w_ffn_conv, w_down, *, tm, tf):
    s = h2.shape[0]
    n_f = D_FF // tf
    return pl.pallas_call(
        functools.partial(_ffn_kernel, tm=tm),
        grid=(s // tm, n_f),
        in_specs=[
            pl.BlockSpec((tm, D_MODEL), lambda i, f: (i, 0)),
            pl.BlockSpec((D_MODEL, tf), lambda i, f: (0, f)),
            pl.BlockSpec((D_MODEL, tf), lambda i, f: (0, n_f + f)),
            pl.BlockSpec((3, tf), lambda i, f: (0, f)),
            pl.BlockSpec((3, tf), lambda i, f: (0, n_f + f)),
            pl.BlockSpec((tf, D_MODEL), lambda i, f: (f, 0)),
        ],
        out_specs=pl.BlockSpec((tm, D_MODEL), lambda i, f: (i, 0)),
        out_shape=jax.ShapeDtypeStruct((s, D_MODEL), F32),
        scratch_shapes=[pltpu.VMEM((n_f, 2, CARRY_ROWS, tf), F32),
                        pltpu.VMEM((2, tf // FFN_SLICE, CARRY_ROWS + tm, FFN_SLICE), F32),
                        pltpu.VMEM((tf // FFN_SLICE, tm, FFN_SLICE), BF16)],
        compiler_params=_params("arbitrary", "arbitrary"),
        name="conv_ffn",
    )(h2, w_up, w_up, w_ffn_conv, w_ffn_conv, w_down)


def _ple_kernel(x_ref, y_ref, p_ref, wple_ref, gple_ref, gpg_ref, wpg_ref, bpg_ref, o_ref):
    xf = x_ref[...] + y_ref[...]
    pe = _rms_rows(jnp.dot(p_ref[...].astype(BF16), wple_ref[...], preferred_element_type=F32),
                   gple_ref[...])
    hn = _rms_rows(xf, gpg_ref[...]).astype(BF16)
    pg = jax.nn.sigmoid(jnp.dot(hn, wpg_ref[...], preferred_element_type=F32) + bpg_ref[...])
    o_ref[...] = xf + pg * pe


def _ple_call(x1, y_ffn, p, w_ple, g_ple, g_pg, w_pg, b_pg, *, tm):
    s = x1.shape[0]
    const = lambda i: (0, 0)
    return pl.pallas_call(
        _ple_kernel,
        grid=(s // tm,),
        in_specs=[
            pl.BlockSpec((tm, D_MODEL), lambda i: (i, 0)),
            pl.BlockSpec((tm, D_MODEL), lambda i: (i, 0)),
            pl.BlockSpec((tm, PLE_DIM), lambda i: (i, 0)),
            _resident((PLE_DIM, D_MODEL), const),
            _resident((1, D_MODEL), const),
            _resident((1, D_MODEL), const),
            _resident((D_MODEL, D_MODEL), const),
            _resident((1, D_MODEL), const),
        ],
        out_specs=pl.BlockSpec((tm, D_MODEL), lambda i: (i, 0)),
        out_shape=jax.ShapeDtypeStruct((s, D_MODEL), F32),
        compiler_params=_params("arbitrary"),
        name="ple_gate",
    )(x1, y_ffn, p, w_ple, g_ple, g_pg, w_pg, b_pg)


def _layer(depth_index, x, p, g_mix, w_in, b_gates, g_q, g_k, lam_rows, g_sub, w_conv_mix, w_att_out,
           w_conv_out, w_o, g_ffn, w_up, w_ffn_conv, w_down, w_ple, g_ple, g_pg, w_pg, b_pg):
    lambda_init = 0.8 - 0.6 * math.exp(-0.3 * depth_index)
    row = lambda a: a.reshape(1, -1).astype(F32)

    g_mix = row(g_mix)
    qkv, y_conv = _inproj_call(x, g_mix, w_in.astype(F32), _qk_table(g_q, g_k),
                               w_conv_mix.astype(F32), tm=1024)
    later = [w_in, w_in, w_att_out, w_conv_out, w_o, w_up, w_down, w_ple, w_pg]
    jobs = ([(D_MODEL, OFF_GA // D_MODEL), (D_MODEL, OFF_GC // D_MODEL)]
            + [(w.shape[1], 0) for w in later[2:]])
    y_att, w_ga, w_gc, w_att_out, w_conv_out, w_o, w_up, w_down, w_ple, w_pg = _attention(
        lam_rows, row(g_sub), g_q, g_k, qkv, [w.astype(F32) for w in later], jobs,
        t=1024, lambda_init=lambda_init)
    x1, h2 = _merge_call(x, g_mix, y_att, y_conv, w_ga, w_gc, b_gates.astype(F32),
                         w_att_out, w_conv_out, w_o, row(g_ffn), tm=256)
    y_ffn = _ffn_call(h2, w_up, w_ffn_conv.astype(F32), w_down, tm=1024, tf=512)
    return _ple_call(x1, y_ffn, p, w_ple, row(g_ple), row(g_pg), w_pg, row(b_pg), tm=512)


def kernel(x, p, g_mix, w_in, b_gates, g_q, g_k, lam_q1, lam_k1, lam_q2, lam_k2, g_sub, w_conv_mix,
           w_att_out, w_conv_out, w_o, g_ffn, w_up, w_ffn_conv, w_down, w_ple, g_ple, g_pg, w_pg,
           b_pg):
    batch, seq, d_model = x.shape
    depth = p.shape[0]
    outs = []
    for b in range(batch):
        xb = x.reshape(seq, d_model) if batch == 1 else x[b]
        for i in range(depth):
            lam_rows = jnp.stack([lam_q1[i], lam_k1[i], lam_q2[i], lam_k2[i]]).astype(F32)
            pb = p.reshape(seq, PLE_DIM) if batch == 1 and depth == 1 else p[i, b]
            xb = _layer(i, xb, pb, g_mix[i], w_in[i], b_gates[i], g_q[i], g_k[i], lam_rows,
                        g_sub[i], w_conv_mix[i], w_att_out[i], w_conv_out[i], w_o[i], g_ffn[i],
                        w_up[i], w_ffn_conv[i], w_down[i], w_ple[i], g_ple[i], g_pg[i], w_pg[i],
                        b_pg[i])
        outs.append(xb)
    return outs[0].reshape(x.shape) if batch == 1 else jnp.stack(outs)
```

```python
import functools
import math

import jax
import jax.numpy as jnp
import numpy as np
from jax import lax
from jax.experimental import pallas as pl
from jax.experimental.pallas import tpu as pltpu

F32 = jnp.float32
BF16 = jnp.bfloat16

D_MODEL = 2048
ATT_HEADS = 8
ATT_HEAD_DIM = 64
ATT_V_DIM = 2 * ATT_HEAD_DIM
ATT_QK_WIDTH = ATT_HEADS * 2 * ATT_HEAD_DIM
ATT_WIDTH = ATT_HEADS * ATT_V_DIM
CONV_WIDTH = 1024
D_FF = 5632
PLE_DIM = 256
EPS = 1e-6

LANES = 128
CARRY_ROWS = 8
MASK_VALUE = -1e30
VMEM_LIMIT = 56 * 1024 * 1024

OFF_GA = 2 * ATT_QK_WIDTH + ATT_WIDTH + 3 * CONV_WIDTH
OFF_GC = OFF_GA + D_MODEL

NT_DIMS = (((1,), (1,)), ((), ()))


def _rms_rows(xf, g):
    ms = jnp.mean(xf * xf, axis=-1, keepdims=True)
    return xf * lax.rsqrt(ms + EPS) * g


def _params(*sem):
    return pltpu.CompilerParams(dimension_semantics=sem, vmem_limit_bytes=VMEM_LIMIT)


def _resident(shape, index_map):
    return pl.BlockSpec(shape, index_map, pipeline_mode=pl.Buffered(1))


def _shift_rows(u, prev, rows):
    u1 = jnp.where(rows == 0, prev[CARRY_ROWS - 1:CARRY_ROWS], pltpu.roll(u, 1, 0))
    u2 = jnp.where(rows == 0, prev[CARRY_ROWS - 2:CARRY_ROWS - 1],
                   jnp.where(rows == 1, prev[CARRY_ROWS - 1:CARRY_ROWS], pltpu.roll(u, 2, 0)))
    return u1, u2


def _causal_conv3(u, w, prev):
    rows = lax.broadcasted_iota(jnp.int32, u.shape, 0)
    u1, u2 = _shift_rows(u, prev, rows)
    return w[0:1] * u2 + w[1:2] * u1 + w[2:3] * u


ROW_GAIN, ROW_ONES, ROW_SLOPE, ROW_HI, ROW_MID, ROW_LO, TABLE_ROWS = 0, 1, 2, 3, 4, 5, 8
V_GROUP = 2 * LANES
PROJ_TILE = 512
N_QK_TILES = 2 * ATT_QK_WIDTH // PROJ_TILE
N_QKV_TILES = N_QK_TILES + ATT_WIDTH // PROJ_TILE
N_CONV_TILES = CONV_WIDTH // PROJ_TILE
N_PROJ_TILES = N_QKV_TILES + 3 * N_CONV_TILES
TILE_V0 = N_QK_TILES
TILE_CB0 = N_QKV_TILES
TILE_CX0 = N_QKV_TILES + 2 * N_CONV_TILES
QKV_OUT_TILE = 2 * PROJ_TILE
QKV_OUT_WIDTH = N_QKV_TILES * QKV_OUT_TILE


def _qk_tile(acc, tab_ref, o_ref, first_pos):
    tm = acc.shape[0]
    lane = lax.broadcasted_iota(jnp.int32, (tm, LANES), 1)
    pos = (first_pos + lax.broadcasted_iota(jnp.int32, (tm, LANES), 0)).astype(F32)
    first = lane < ATT_HEAD_DIM
    for c in range(acc.shape[1] // LANES):
        t = acc[:, c * LANES:(c + 1) * LANES]
        sl = slice(c * LANES, (c + 1) * LANES)
        sq = t * t
        ss0 = jnp.sum(jnp.where(first, sq, 0.0), axis=-1, keepdims=True)
        ss1 = jnp.sum(jnp.where(first, 0.0, sq), axis=-1, keepdims=True)
        r = jnp.where(first, lax.rsqrt(ss0 / ATT_HEAD_DIM + EPS), lax.rsqrt(ss1 / ATT_HEAD_DIM + EPS))
        tnorm = t * r * tab_ref[ROW_GAIN:ROW_GAIN + 1, sl]
        bias = tab_ref[ROW_SLOPE:ROW_SLOPE + 1, sl] * pos
        hi = bias.astype(BF16).astype(F32)
        rem = bias - hi
        mid = rem.astype(BF16).astype(F32)
        lo = rem - mid
        aug = (tab_ref[ROW_ONES:ROW_ONES + 1, sl] + tab_ref[ROW_HI:ROW_HI + 1, sl] * hi
               + tab_ref[ROW_MID:ROW_MID + 1, sl] * mid + tab_ref[ROW_LO:ROW_LO + 1, sl] * lo)
        comp0 = jnp.where(first, tnorm, aug)
        comp1 = jnp.where(first, pltpu.roll(tnorm, ATT_HEAD_DIM, 1), aug)
        out0 = 2 * c * LANES
        o_ref[:, out0:out0 + LANES] = comp0.astype(BF16)
        o_ref[:, out0 + LANES:out0 + 2 * LANES] = comp1.astype(BF16)


def _v_tile(acc, o_ref):
    tm = acc.shape[0]
    ones = jnp.ones((tm, V_GROUP - ATT_V_DIM), BF16)
    for a in range(PROJ_TILE // ATT_V_DIM):
        o_ref[:, a * V_GROUP:a * V_GROUP + ATT_V_DIM] = acc[:, a * ATT_V_DIM:(a + 1) * ATT_V_DIM].astype(BF16)
        o_ref[:, a * V_GROUP + ATT_V_DIM:(a + 1) * V_GROUP] = ones


def _inproj_kernel(x_ref, g_ref, w_ref, tab_ref, cw_ref, qkv_ref, conv_ref,
                   h_ref, carry_ref, park_ref, hold_ref, *, tm):
    i = pl.program_id(0)
    j = pl.program_id(1)

    def project(dst_ref, slot):
        dst_ref[slot] = jnp.dot(h_ref[...], w_ref[...].astype(BF16), preferred_element_type=F32)

    def finish_conv(jc, cx_slot):
        z = hold_ref[N_CONV_TILES + jc] * park_ref[cx_slot]
        y = _causal_conv3(z, cw_ref[...], carry_ref[jc])
        conv_ref[...] = (hold_ref[jc] * y).astype(BF16)
        carry_ref[jc] = z[tm - CARRY_ROWS:tm]

    @pl.when(j == 0)
    def _():
        @pl.when(i == 0)
        def _():
            carry_ref[...] = jnp.zeros(carry_ref.shape, F32)

        h_ref[...] = _rms_rows(x_ref[...], g_ref[...]).astype(BF16)
        project(park_ref, 0)

    @pl.when(jnp.logical_and(j >= 1, j <= TILE_V0))
    def _():
        _qk_tile(park_ref[(j - 1) % 2], tab_ref, qkv_ref, i * tm)
        project(park_ref, j % 2)

    @pl.when(j == TILE_V0 + 1)
    def _():
        _v_tile(park_ref[TILE_V0 % 2], qkv_ref)
        project(park_ref, (TILE_V0 + 1) % 2)

    @pl.when(j == TILE_CB0)
    def _():
        _v_tile(park_ref[(TILE_V0 + 1) % 2], qkv_ref)
        project(hold_ref, 0)

    @pl.when(jnp.logical_and(j > TILE_CB0, j < TILE_CX0))
    def _():
        project(hold_ref, j - TILE_CB0)

    @pl.when(j == TILE_CX0)
    def _():
        project(park_ref, 0)

    @pl.when(j == TILE_CX0 + 1)
    def _():
        finish_conv(0, 0)
        project(park_ref, 1)

    @pl.when(j == TILE_CX0 + 2)
    def _():
        finish_conv(1, 1)


def _qk_table(g_q, g_k):
    n_groups = ATT_QK_WIDTH // ATT_HEAD_DIM
    slopes = jnp.asarray(2.0 ** (-8.0 * np.arange(1, ATT_HEADS + 1) / ATT_HEADS), F32)
    slope_cols = jnp.repeat(slopes, 2 * ATT_HEAD_DIM)
    lane = jnp.tile(jnp.arange(LANES), ATT_HEADS) - ATT_HEAD_DIM
    at = lambda n: (lane == n).astype(F32)
    zeros = jnp.zeros(ATT_QK_WIDTH, F32)
    both = lambda q_part, k_part: jnp.concatenate([q_part, k_part])
    rows = [None] * TABLE_ROWS
    rows[ROW_GAIN] = both(jnp.tile(g_q.astype(F32) * ATT_HEAD_DIM ** -0.5, n_groups),
                          jnp.tile(g_k.astype(F32), n_groups))
    rows[ROW_ONES] = both(at(0) + at(1) + at(2), at(3) + at(4) + at(5))
    rows[ROW_SLOPE] = both(-slope_cols, slope_cols)
    rows[ROW_HI] = both(at(3), at(0))
    rows[ROW_MID] = both(at(4), at(1))
    rows[ROW_LO] = both(at(5), at(2))
    return jnp.stack([both(zeros, zeros) if r is None else r for r in rows])


def _inproj_call(x, g_mix, w_in, table, w_conv, *, tm):
    s = x.shape[0]
    finished = lambda j, first, count: jnp.clip(j - 1 - first, 0, count - 1)
    return pl.pallas_call(
        functools.partial(_inproj_kernel, tm=tm),
        grid=(s // tm, N_PROJ_TILES + 1),
        in_specs=[
            pl.BlockSpec((tm, D_MODEL), lambda i, j: (i, 0)),
            pl.BlockSpec((1, D_MODEL), lambda i, j: (0, 0)),
            pl.BlockSpec((D_MODEL, PROJ_TILE), lambda i, j: (0, jnp.minimum(j, N_PROJ_TILES - 1))),
            pl.BlockSpec((TABLE_ROWS, PROJ_TILE), lambda i, j: (0, finished(j, 0, N_QK_TILES))),
            pl.BlockSpec((3, PROJ_TILE), lambda i, j: (0, finished(j, TILE_CX0, N_CONV_TILES))),
        ],
        out_specs=[pl.BlockSpec((tm, QKV_OUT_TILE), lambda i, j: (i, finished(j, 0, N_QKV_TILES))),
                   pl.BlockSpec((tm, PROJ_TILE),
                                lambda i, j: (i, finished(j, TILE_CX0, N_CONV_TILES)))],
        out_shape=[jax.ShapeDtypeStruct((s, QKV_OUT_WIDTH), BF16),
                   jax.ShapeDtypeStruct((s, CONV_WIDTH), BF16)],
        scratch_shapes=[pltpu.VMEM((tm, D_MODEL), BF16),
                        pltpu.VMEM((N_CONV_TILES, CARRY_ROWS, PROJ_TILE), F32),
                        pltpu.VMEM((2, tm, PROJ_TILE), F32),
                        pltpu.VMEM((2 * N_CONV_TILES, tm, PROJ_TILE), F32)],
        compiler_params=_params("arbitrary", "arbitrary"),
        name="in_proj",
    )(x, g_mix, w_in, table, w_conv)


def _attn_kernel(lam_ref, gsub_ref, q0_ref, q1_ref, k0_ref, k1_ref, v_ref, o_ref,
                 m_ref, l_ref, acc_ref, *, t, lambda_init, side_work):
    i = pl.program_id(1)
    side_work()
    m_ref[...] = jnp.full(m_ref.shape, MASK_VALUE, F32)
    l_ref[...] = jnp.zeros(l_ref.shape, F32)
    acc_ref[...] = jnp.zeros(acc_ref.shape, F32)

    q = (q0_ref[...], q1_ref[...])
    k_refs = (k0_ref, k1_ref)
    n_chunks = t // LANES

    def tile(j, masked):
        start = pl.multiple_of(j * t, t)
        v = v_ref[pl.ds(start, t), :]
        if masked:
            keep = (lax.broadcasted_iota(jnp.int32, (t, t), 1)
                    <= lax.broadcasted_iota(jnp.int32, (t, t), 0))
        for c in range(2):
            k = k_refs[c][pl.ds(start, t), :]
            s = lax.dot_general(q[c], k, NT_DIMS, preferred_element_type=F32)
            if masked:
                s = jnp.where(keep, s, MASK_VALUE)
            chunks = [s[:, a * LANES:(a + 1) * LANES] for a in range(n_chunks)]
            m_prev = m_ref[c]
            m_new = jnp.maximum(m_prev, jnp.max(functools.reduce(jnp.maximum, chunks),
                                                axis=1, keepdims=True))
            alpha = jnp.exp(m_prev - m_new)
            ps = [jnp.exp(ch - m_new) for ch in chunks]
            l_ref[c] = alpha * l_ref[c] + functools.reduce(jnp.add, ps)
            p = jnp.concatenate([x.astype(BF16) for x in ps], axis=1)
            acc_ref[c] = alpha * acc_ref[c] + jnp.dot(p, v, preferred_element_type=F32)
            m_ref[c] = m_new

    def body(j, carry):
        tile(j, False)
        return carry

    lax.fori_loop(0, i, body, 0)
    tile(i, True)

    l0 = jnp.sum(l_ref[0], axis=1, keepdims=True)
    l1 = jnp.sum(l_ref[1], axis=1, keepdims=True)
    _attn_finish(acc_ref[0], l0, acc_ref[1], l1, lam_ref, gsub_ref, o_ref, lambda_init)


def _attn_finish(acc0, l0, acc1, l1, lam_ref, gsub_ref, o_ref, lambda_init):
    lam_rows = lam_ref[...]
    lam = (jnp.exp(jnp.sum(lam_rows[0:1] * lam_rows[1:2], axis=-1, keepdims=True))
           - jnp.exp(jnp.sum(lam_rows[2:3] * lam_rows[3:4], axis=-1, keepdims=True))
           + lambda_init)
    o = acc0 / l0 - lam * (acc1 / l1)
    o_ref[...] = (_rms_rows(o, gsub_ref[...]) * (1.0 - lambda_init)).astype(BF16)


def _attn_bounded_kernel(lam_ref, gsub_ref, q0_ref, q1_ref, k0_ref, k1_ref, v_ref, o_ref, acc_ref,
                         *, t, lambda_init, wide, side_work):
    i = pl.program_id(1)
    q = (q0_ref[...], q1_ref[...])
    k_refs = (k0_ref, k1_ref)

    def tile(first, n_tiles, diagonal_last):
        start = pl.multiple_of(first * t, t)
        n_keys = n_tiles * t
        v = v_ref[pl.ds(start, n_keys), :]
        for c in range(2):
            k = k_refs[c][pl.ds(start, n_keys), :]
            s = lax.dot_general(q[c], k, NT_DIMS, preferred_element_type=F32)
            if diagonal_last:
                keep = (lax.broadcasted_iota(jnp.int32, (t, t), 1)
                        <= lax.broadcasted_iota(jnp.int32, (t, t), 0))
                parts = [jnp.where(keep, s[:, n_keys - t:], MASK_VALUE)]
                if n_tiles > 1:
                    parts.insert(0, s[:, :n_keys - t])
                s = jnp.concatenate(parts, axis=1)
            pv = jnp.dot(jnp.exp(s).astype(BF16), v, preferred_element_type=F32)
            if diagonal_last:
                acc_ref[c] = pv
            else:
                acc_ref[c] += pv

    n_wide = i // wide
    for r in range(wide):
        @pl.when(i - n_wide * wide == r)
        def _(r=r):
            side_work()
            tile(i - r, r + 1, True)

    def wide_body(j, carry):
        tile(j * wide, wide, False)
        return carry

    lax.fori_loop(0, n_wide, wide_body, 0)

    acc0 = acc_ref[0]
    acc1 = acc_ref[1]
    _attn_finish(acc0[:, :ATT_V_DIM], acc0[:, ATT_V_DIM:],
                 acc1[:, :ATT_V_DIM], acc1[:, ATT_V_DIM:],
                 lam_ref, gsub_ref, o_ref, lambda_init)


ATT_WIDE_TILES = 4


BF16_ROWS = 16
N_ATTN_INPUTS = 7


def _with_weight_casts(attn_body, n):
    def kernel(*refs):
        ins, refs = refs[:N_ATTN_INPUTS], refs[N_ATTN_INPUTS:]
        cast_in, o_ref, cast_out, scratch = refs[:n], refs[n], refs[n + 1:2 * n + 1], refs[2 * n + 1:]

        def cast_weights():
            for src, dst in zip(cast_in, cast_out):
                dst[...] = src[...].astype(BF16)

        attn_body(*ins, o_ref, *scratch, side_work=cast_weights)

    return kernel


def _attn_call(lam_rows, g_sub, qkv, *cast_arrays, cast_jobs, t, lambda_init, bounded):
    s = qkv.shape[0]
    n_q = s // t
    n_steps = ATT_HEADS * n_q
    k_off = 2 * ATT_HEADS
    v_off = 2 * ATT_QK_WIDTH // ATT_HEAD_DIM
    if bounded:
        body = functools.partial(_attn_bounded_kernel, wide=ATT_WIDE_TILES)
        v_spec = pl.BlockSpec((s, V_GROUP), lambda h, i: (0, v_off * LANES // V_GROUP + h))
        scratch = [pltpu.VMEM((2, t, V_GROUP), F32)]
    else:
        body = _attn_kernel
        v_spec = pl.BlockSpec((s, ATT_V_DIM), lambda h, i: (0, v_off + (V_GROUP // ATT_V_DIM) * h))
        scratch = [pltpu.VMEM((2, t, LANES), F32),
                   pltpu.VMEM((2, t, LANES), F32),
                   pltpu.VMEM((2, t, ATT_V_DIM), F32)]
    cast_in_specs, cast_out_specs, cast_out_shapes = [], [], []
    for arr, (width, col_block) in zip(cast_arrays, cast_jobs):
        rows = arr.shape[0]
        col_blocks = 1
        while rows % (BF16_ROWS * n_steps // col_blocks):
            col_blocks *= 2
        block = (rows * col_blocks // n_steps, width // col_blocks)
        assert block[1] % LANES == 0 and block[1] * col_blocks == width
        cast_in_specs.append(pl.BlockSpec(
            block, lambda h, i, cb=col_blocks, c0=col_block * col_blocks:
            ((h * n_q + i) // cb, c0 + (h * n_q + i) % cb)))
        cast_out_specs.append(pl.BlockSpec(
            block, lambda h, i, cb=col_blocks: ((h * n_q + i) // cb, (h * n_q + i) % cb)))
        cast_out_shapes.append(jax.ShapeDtypeStruct((rows, width), BF16))
    return pl.pallas_call(
        _with_weight_casts(functools.partial(body, t=t, lambda_init=lambda_init), len(cast_jobs)),
        grid=(ATT_HEADS, n_q),
        in_specs=[
            pl.BlockSpec((4, ATT_HEAD_DIM), lambda h, i: (0, 0)),
            pl.BlockSpec((1, ATT_V_DIM), lambda h, i: (0, 0)),
            pl.BlockSpec((t, LANES), lambda h, i: (i, 2 * h)),
            pl.BlockSpec((t, LANES), lambda h, i: (i, 2 * h + 1)),
            pl.BlockSpec((s, LANES), lambda h, i: (0, k_off + 2 * h)),
            pl.BlockSpec((s, LANES), lambda h, i: (0, k_off + 2 * h + 1)),
            v_spec,
        ] + cast_in_specs,
        out_specs=[pl.BlockSpec((t, ATT_V_DIM), lambda h, i: (i, h))] + cast_out_specs,
        out_shape=[jax.ShapeDtypeStruct((s, ATT_WIDTH), BF16)] + cast_out_shapes,
        scratch_shapes=scratch,
        compiler_params=_params("arbitrary", "arbitrary"),
        name="diff_attn_bounded" if bounded else "diff_attn",
    )(lam_rows, g_sub, qkv, qkv, qkv, qkv, qkv, *cast_arrays)


SCORE_BOUND_LIMIT = 0.0


def _attention(lam_rows, g_sub, g_q, g_k, qkv, cast_arrays, cast_jobs, *, t, lambda_init):
    bound = (1.01 * ATT_HEAD_DIM ** 0.5
             * jnp.max(jnp.abs(g_q.astype(F32))) * jnp.max(jnp.abs(g_k.astype(F32))))
    call = functools.partial(_attn_call, cast_jobs=cast_jobs, t=t, lambda_init=lambda_init)
    return lax.cond(bound <= SCORE_BOUND_LIMIT,
                    functools.partial(call, bounded=True),
                    functools.partial(call, bounded=False),
                    lam_rows, g_sub, qkv, *cast_arrays)


def _merge_kernel(x_ref, g_ref, ya_ref, yc_ref, wga_ref, wgc_ref, bg_ref, wa_ref, wc_ref, wo_ref,
                  gffn_ref, x1_ref, h2_ref):
    xf = x_ref[...]
    h = _rms_rows(xf, g_ref[...]).astype(BF16)
    bg = bg_ref[...]
    gate_a = jax.nn.sigmoid(jnp.dot(h, wga_ref[...], preferred_element_type=F32) + bg[0:1])
    merged = gate_a * jnp.dot(ya_ref[...], wa_ref[...], preferred_element_type=F32)
    gate_c = jax.nn.sigmoid(jnp.dot(h, wgc_ref[...], preferred_element_type=F32) + bg[1:2])
    merged = merged + gate_c * jnp.dot(yc_ref[...], wc_ref[...], preferred_element_type=F32)
    x1 = xf + jnp.dot(merged.astype(BF16), wo_ref[...], preferred_element_type=F32)
    x1_ref[...] = x1
    h2_ref[...] = _rms_rows(x1, gffn_ref[...]).astype(BF16)


def _merge_call(x, g_mix, y_att, y_conv, w_ga, w_gc, b_gates, w_att_out, w_conv_out, w_o, g_ffn,
                *, tm):
    s = x.shape[0]
    const = lambda i: (0, 0)
    return pl.pallas_call(
        _merge_kernel,
        grid=(s // tm,),
        in_specs=[
            pl.BlockSpec((tm, D_MODEL), lambda i: (i, 0)),
            _resident((1, D_MODEL), const),
            pl.BlockSpec((tm, ATT_WIDTH), lambda i: (i, 0)),
            pl.BlockSpec((tm, CONV_WIDTH), lambda i: (i, 0)),
            _resident((D_MODEL, D_MODEL), const),
            _resident((D_MODEL, D_MODEL), const),
            _resident((2, D_MODEL), const),
            _resident((ATT_WIDTH, D_MODEL), const),
            _resident((CONV_WIDTH, D_MODEL), const),
            _resident((D_MODEL, D_MODEL), const),
            _resident((1, D_MODEL), const),
        ],
        out_specs=[pl.BlockSpec((tm, D_MODEL), lambda i: (i, 0)),
                   pl.BlockSpec((tm, D_MODEL), lambda i: (i, 0))],
        out_shape=[jax.ShapeDtypeStruct((s, D_MODEL), F32),
                   jax.ShapeDtypeStruct((s, D_MODEL), BF16)],
        compiler_params=_params("arbitrary"),
        name="merge",
    )(x, g_mix, y_att, y_conv, w_ga, w_gc, b_gates, w_att_out, w_conv_out, w_o, g_ffn)


FFN_SLICE = 256


def _ffn_kernel(h_ref, wg_ref, wv_ref, cwg_ref, cwv_ref, wd_ref, o_ref,
                carry_ref, pre_ref, act_ref, *, tm):
    i = pl.program_id(0)
    f = pl.program_id(1)

    @pl.when(i == 0)
    def _():
        carry_ref[f] = jnp.zeros(carry_ref.shape[1:], F32)

    @pl.when(f == 0)
    def _():
        o_ref[...] = jnp.zeros(o_ref.shape, F32)

    h = h_ref[...]
    n_slices = wg_ref.shape[1] // FFN_SLICE
    slices = [slice(a * FFN_SLICE, (a + 1) * FFN_SLICE) for a in range(n_slices)]
    for a, sl in enumerate(slices):
        for b, (w_ref, cw_ref) in enumerate(((wg_ref, cwg_ref), (wv_ref, cwv_ref))):
            pre_ref[b, a, 0:CARRY_ROWS] = carry_ref[f, b, :, sl]
            pre_ref[b, a, CARRY_ROWS:CARRY_ROWS + tm] = jnp.dot(h, w_ref[:, sl],
                                                                preferred_element_type=F32)
            carry_ref[f, b, :, sl] = pre_ref[b, a, tm:tm + CARRY_ROWS]
    for a, sl in enumerate(slices):
        u = []
        for b, cw_ref in enumerate((cwg_ref, cwv_ref)):
            w = cw_ref[:, sl]
            u.append(w[0:1] * pre_ref[b, a, CARRY_ROWS - 2:CARRY_ROWS - 2 + tm]
                     + w[1:2] * pre_ref[b, a, CARRY_ROWS - 1:CARRY_ROWS - 1 + tm]
                     + w[2:3] * pre_ref[b, a, CARRY_ROWS:CARRY_ROWS + tm])
        act_ref[a] = (u[0] * jax.nn.sigmoid(u[0]) * u[1]).astype(BF16)
    for a, sl in enumerate(slices):
        for n in range(o_ref.shape[1] // FFN_SLICE):
            cs = slice(n * FFN_SLICE, (n + 1) * FFN_SLICE)
            o_ref[:, cs] += jnp.dot(act_ref[a], wd_ref[sl, cs], preferred_element_type=F32)


def _ffn_call(h2, w_up, w_ffn_conv, w_down, *, tm, tf):
    s = h2.shape[0]
    n_f = D_FF // tf
    return pl.pallas_call(
        functools.partial(_ffn_kernel, tm=tm),
        grid=(s // tm, n_f),
        in_specs=[
            pl.BlockSpec((tm, D_MODEL), lambda i, f: (i, 0)),
            pl.BlockSpec((D_MODEL, tf), lambda i, f: (0, f)),
            pl.BlockSpec((D_MODEL, tf), lambda i, f: (0, n_f + f)),
            pl.BlockSpec((3, tf), lambda i, f: (0, f)),
            pl.BlockSpec((3, tf), lambda i, f: (0, n_f + f)),
            pl.BlockSpec((tf, D_MODEL), lambda i, f: (f, 0)),
        ],
        out_specs=pl.BlockSpec((tm, D_MODEL), lambda i, f: (i, 0)),
        out_shape=jax.ShapeDtypeStruct((s, D_MODEL), F32),
        scratch_shapes=[pltpu.VMEM((n_f, 2, CARRY_ROWS, tf), F32),
                        pltpu.VMEM((2, tf // FFN_SLICE, CARRY_ROWS + tm, FFN_SLICE), F32),
                        pltpu.VMEM((tf // FFN_SLICE, tm, FFN_SLICE), BF16)],
        compiler_params=_params("arbitrary", "arbitrary"),
        name="conv_ffn",
    )(h2, w_up, w_up, w_ffn_conv, w_ffn_conv, w_down)


def _ple_kernel(x_ref, y_ref, p_ref, wple_ref, gple_ref, gpg_ref, wpg_ref, bpg_ref, o_ref):
    xf = x_ref[...] + y_ref[...]
    pe = _rms_rows(jnp.dot(p_ref[...].astype(BF16), wple_ref[...], preferred_element_type=F32),
                   gple_ref[...])
    hn = _rms_rows(xf, gpg_ref[...]).astype(BF16)
    pg = jax.nn.sigmoid(jnp.dot(hn, wpg_ref[...], preferred_element_type=F32) + bpg_ref[...])
    o_ref[...] = xf + pg * pe


def _ple_call(x1, y_ffn, p, w_ple, g_ple, g_pg, w_pg, b_pg, *, tm):
    s = x1.shape[0]
    const = lambda i: (0, 0)
    return pl.pallas_call(
        _ple_kernel,
        grid=(s // tm,),
        in_specs=[
            pl.BlockSpec((tm, D_MODEL), lambda i: (i, 0)),
            pl.BlockSpec((tm, D_MODEL), lambda i: (i, 0)),
            pl.BlockSpec((tm, PLE_DIM), lambda i: (i, 0)),
            _resident((PLE_DIM, D_MODEL), const),
            _resident((1, D_MODEL), const),
            _resident((1, D_MODEL), const),
            _resident((D_MODEL, D_MODEL), const),
            _resident((1, D_MODEL), const),
        ],
        out_specs=pl.BlockSpec((tm, D_MODEL), lambda i: (i, 0)),
        out_shape=jax.ShapeDtypeStruct((s, D_MODEL), F32),
        compiler_params=_params("arbitrary"),
        name="ple_gate",
    )(x1, y_ffn, p, w_ple, g_ple, g_pg, w_pg, b_pg)


def _layer(depth_index, x, p, g_mix, w_in, b_gates, g_q, g_k, lam_rows, g_sub, w_conv_mix, w_att_out,
           w_conv_out, w_o, g_ffn, w_up, w_ffn_conv, w_down, w_ple, g_ple, g_pg, w_pg, b_pg):
    lambda_init = 0.8 - 0.6 * math.exp(-0.3 * depth_index)
    row = lambda a: a.reshape(1, -1).astype(F32)

    g_mix = row(g_mix)
    qkv, y_conv = _inproj_call(x, g_mix, w_in.astype(F32), _qk_table(g_q, g_k),
                               w_conv_mix.astype(F32), tm=1024)
    later = [w_in, w_in, w_att_out, w_conv_out, w_o, w_up, w_down, w_ple, w_pg]
    jobs = ([(D_MODEL, OFF_GA // D_MODEL), (D_MODEL, OFF_GC // D_MODEL)]
            + [(w.shape[1], 0) for w in later[2:]])
    y_att, w_ga, w_gc, w_att_out, w_conv_out, w_o, w_up, w_down, w_ple, w_pg = _attention(
        lam_rows, row(g_sub), g_q, g_k, qkv, [w.astype(F32) for w in later], jobs,
        t=1024, lambda_init=lambda_init)
    x1, h2 = _merge_call(x, g_mix, y_att, y_conv, w_ga, w_gc, b_gates.astype(F32),
                         w_att_out, w_conv_out, w_o, row(g_ffn), tm=256)
    y_ffn = _ffn_call(h2, w_up, w_ffn_conv.astype(F32), w_down, tm=1024, tf=512)
    return _ple_call(x1, y_ffn, p, w_ple, row(g_ple), row(g_pg), w_pg, row(b_pg), tm=512)


def kernel(x, p, g_mix, w_in, b_gates, g_q, g_k, lam_q1, lam_k1, lam_q2, lam_k2, g_sub, w_conv_mix,
           w_att_out, w_conv_out, w_o, g_ffn, w_up, w_ffn_conv, w_down, w_ple, g_ple, g_pg, w_pg,
           b_pg):
    batch, seq, d_model = x.shape
    depth = p.shape[0]
    outs = []
    for b in range(batch):
        xb = x.reshape(seq, d_model) if batch == 1 else x[b]
        for i in range(depth):
            lam_rows = jnp.stack([lam_q1[i], lam_k1[i], lam_q2[i], lam_k2[i]]).astype(F32)
            pb = p.reshape(seq, PLE_DIM) if batch == 1 and depth == 1 else p[i, b]
            xb = _layer(i, xb, pb, g_mix[i], w_in[i], b_gates[i], g_q[i], g_k[i], lam_rows,
                        g_sub[i], w_conv_mix[i], w_att_out[i], w_conv_out[i], w_o[i], g_ffn[i],
                        w_up[i], w_ffn_conv[i], w_down[i], w_ple[i], g_ple[i], g_pg[i], w_pg[i],
                        b_pg[i])
        outs.append(xb)
    return outs[0].reshape(x.shape) if batch == 1 else jnp.stack(outs)
```

```python
import functools
import math

import jax
import jax.numpy as jnp
import numpy as np
from jax import lax
from jax.experimental import pallas as pl
from jax.experimental.pallas import tpu as pltpu

F32 = jnp.float32
BF16 = jnp.bfloat16

D_MODEL = 2048
ATT_HEADS = 8
ATT_HEAD_DIM = 64
ATT_V_DIM = 2 * ATT_HEAD_DIM
ATT_QK_WIDTH = ATT_HEADS * 2 * ATT_HEAD_DIM
ATT_WIDTH = ATT_HEADS * ATT_V_DIM
CONV_WIDTH = 1024
D_FF = 5632
PLE_DIM = 256
EPS = 1e-6

LANES = 128
CARRY_ROWS = 8
MASK_VALUE = -1e30
VMEM_LIMIT = 56 * 1024 * 1024

OFF_GA = 2 * ATT_QK_WIDTH + ATT_WIDTH + 3 * CONV_WIDTH
OFF_GC = OFF_GA + D_MODEL

NT_DIMS = (((1,), (1,)), ((), ()))


def _rms_rows(xf, g):
    ms = jnp.mean(xf * xf, axis=-1, keepdims=True)
    return xf * lax.rsqrt(ms + EPS) * g


def _params(*sem):
    return pltpu.CompilerParams(dimension_semantics=sem, vmem_limit_bytes=VMEM_LIMIT)


def _resident(shape, index_map):
    return pl.BlockSpec(shape, index_map, pipeline_mode=pl.Buffered(1))


def _shift_rows(u, prev, rows):
    u1 = jnp.where(rows == 0, prev[CARRY_ROWS - 1:CARRY_ROWS], pltpu.roll(u, 1, 0))
    u2 = jnp.where(rows == 0, prev[CARRY_ROWS - 2:CARRY_ROWS - 1],
                   jnp.where(rows == 1, prev[CARRY_ROWS - 1:CARRY_ROWS], pltpu.roll(u, 2, 0)))
    return u1, u2


def _causal_conv3(u, w, prev):
    rows = lax.broadcasted_iota(jnp.int32, u.shape, 0)
    u1, u2 = _shift_rows(u, prev, rows)
    return w[0:1] * u2 + w[1:2] * u1 + w[2:3] * u


ROW_GAIN, ROW_ONES, ROW_SLOPE, ROW_HI, ROW_MID, ROW_LO, TABLE_ROWS = 0, 1, 2, 3, 4, 5, 8
V_GROUP = 2 * LANES
PROJ_TILE = 512
N_QK_TILES = 2 * ATT_QK_WIDTH // PROJ_TILE
N_QKV_TILES = N_QK_TILES + ATT_WIDTH // PROJ_TILE
N_CONV_TILES = CONV_WIDTH // PROJ_TILE
N_PROJ_TILES = N_QKV_TILES + 3 * N_CONV_TILES
TILE_V0 = N_QK_TILES
TILE_CB0 = N_QKV_TILES
TILE_CX0 = N_QKV_TILES + 2 * N_CONV_TILES
QKV_OUT_TILE = 2 * PROJ_TILE
QKV_OUT_WIDTH = N_QKV_TILES * QKV_OUT_TILE


def _qk_tile(acc, tab_ref, o_ref, first_pos):
    tm = acc.shape[0]
    lane = lax.broadcasted_iota(jnp.int32, (tm, LANES), 1)
    pos = (first_pos + lax.broadcasted_iota(jnp.int32, (tm, LANES), 0)).astype(F32)
    first = lane < ATT_HEAD_DIM
    for c in range(acc.shape[1] // LANES):
        t = acc[:, c * LANES:(c + 1) * LANES]
        sl = slice(c * LANES, (c + 1) * LANES)
        sq = t * t
        ss0 = jnp.sum(jnp.where(first, sq, 0.0), axis=-1, keepdims=True)
        ss1 = jnp.sum(jnp.where(first, 0.0, sq), axis=-1, keepdims=True)
        r = jnp.where(first, lax.rsqrt(ss0 / ATT_HEAD_DIM + EPS), lax.rsqrt(ss1 / ATT_HEAD_DIM + EPS))
        tnorm = t * r * tab_ref[ROW_GAIN:ROW_GAIN + 1, sl]
        bias = tab_ref[ROW_SLOPE:ROW_SLOPE + 1, sl] * pos
        hi = bias.astype(BF16).astype(F32)
        rem = bias - hi
        mid = rem.astype(BF16).astype(F32)
        lo = rem - mid
        aug = (tab_ref[ROW_ONES:ROW_ONES + 1, sl] + tab_ref[ROW_HI:ROW_HI + 1, sl] * hi
               + tab_ref[ROW_MID:ROW_MID + 1, sl] * mid + tab_ref[ROW_LO:ROW_LO + 1, sl] * lo)
        comp0 = jnp.where(first, tnorm, aug)
        comp1 = jnp.where(first, pltpu.roll(tnorm, ATT_HEAD_DIM, 1), aug)
        out0 = 2 * c * LANES
        o_ref[:, out0:out0 + LANES] = comp0.astype(BF16)
        o_ref[:, out0 + LANES:out0 + 2 * LANES] = comp1.astype(BF16)


def _v_tile(acc, o_ref):
    tm = acc.shape[0]
    ones = jnp.ones((tm, V_GROUP - ATT_V_DIM), BF16)
    for a in range(PROJ_TILE // ATT_V_DIM):
        o_ref[:, a * V_GROUP:a * V_GROUP + ATT_V_DIM] = acc[:, a * ATT_V_DIM:(a + 1) * ATT_V_DIM].astype(BF16)
        o_ref[:, a * V_GROUP + ATT_V_DIM:(a + 1) * V_GROUP] = ones


def _inproj_kernel(x_ref, g_ref, w_ref, tab_ref, cw_ref, qkv_ref, conv_ref,
                   h_ref, carry_ref, park_ref, hold_ref, *, tm):
    i = pl.program_id(0)
    j = pl.program_id(1)

    def project(dst_ref, slot):
        dst_ref[slot] = jnp.dot(h_ref[...], w_ref[...].astype(BF16), preferred_element_type=F32)

    def finish_conv(jc, cx_slot):
        z = hold_ref[N_CONV_TILES + jc] * park_ref[cx_slot]
        y = _causal_conv3(z, cw_ref[...], carry_ref[jc])
        conv_ref[...] = (hold_ref[jc] * y).astype(BF16)
        carry_ref[jc] = z[tm - CARRY_ROWS:tm]

    @pl.when(j == 0)
    def _():
        @pl.when(i == 0)
        def _():
            carry_ref[...] = jnp.zeros(carry_ref.shape, F32)

        h_ref[...] = _rms_rows(x_ref[...], g_ref[...]).astype(BF16)
        project(park_ref, 0)

    @pl.when(jnp.logical_and(j >= 1, j <= TILE_V0))
    def _():
        _qk_tile(park_ref[(j - 1) % 2], tab_ref, qkv_ref, i * tm)
        project(park_ref, j % 2)

    @pl.when(j == TILE_V0 + 1)
    def _():
        _v_tile(park_ref[TILE_V0 % 2], qkv_ref)
        project(park_ref, (TILE_V0 + 1) % 2)

    @pl.when(j == TILE_CB0)
    def _():
        _v_tile(park_ref[(TILE_V0 + 1) % 2], qkv_ref)
        project(hold_ref, 0)

    @pl.when(jnp.logical_and(j > TILE_CB0, j < TILE_CX0))
    def _():
        project(hold_ref, j - TILE_CB0)

    @pl.when(j == TILE_CX0)
    def _():
        project(park_ref, 0)

    @pl.when(j == TILE_CX0 + 1)
    def _():
        finish_conv(0, 0)
        project(park_ref, 1)

    @pl.when(j == TILE_CX0 + 2)
    def _():
        finish_conv(1, 1)


def _qk_table(g_q, g_k):
    n_groups = ATT_QK_WIDTH // ATT_HEAD_DIM
    slopes = jnp.asarray(2.0 ** (-8.0 * np.arange(1, ATT_HEADS + 1) / ATT_HEADS), F32)
    slope_cols = jnp.repeat(slopes, 2 * ATT_HEAD_DIM)
    lane = jnp.tile(jnp.arange(LANES), ATT_HEADS) - ATT_HEAD_DIM
    at = lambda n: (lane == n).astype(F32)
    zeros = jnp.zeros(ATT_QK_WIDTH, F32)
    both = lambda q_part, k_part: jnp.concatenate([q_part, k_part])
    rows = [None] * TABLE_ROWS
    rows[ROW_GAIN] = both(jnp.tile(g_q.astype(F32) * ATT_HEAD_DIM ** -0.5, n_groups),
                          jnp.tile(g_k.astype(F32), n_groups))
    rows[ROW_ONES] = both(at(0) + at(1) + at(2), at(3) + at(4) + at(5))
    rows[ROW_SLOPE] = both(-slope_cols, slope_cols)
    rows[ROW_HI] = both(at(3), at(0))
    rows[ROW_MID] = both(at(4), at(1))
    rows[ROW_LO] = both(at(5), at(2))
    return jnp.stack([both(zeros, zeros) if r is None else r for r in rows])


def _inproj_call(x, g_mix, w_in, table, w_conv, *, tm):
    s = x.shape[0]
    finished = lambda j, first, count: jnp.clip(j - 1 - first, 0, count - 1)
    return pl.pallas_call(
        functools.partial(_inproj_kernel, tm=tm),
        grid=(s // tm, N_PROJ_TILES + 1),
        in_specs=[
            pl.BlockSpec((tm, D_MODEL), lambda i, j: (i, 0)),
            pl.BlockSpec((1, D_MODEL), lambda i, j: (0, 0)),
            pl.BlockSpec((D_MODEL, PROJ_TILE), lambda i, j: (0, jnp.minimum(j, N_PROJ_TILES - 1))),
            pl.BlockSpec((TABLE_ROWS, PROJ_TILE), lambda i, j: (0, finished(j, 0, N_QK_TILES))),
            pl.BlockSpec((3, PROJ_TILE), lambda i, j: (0, finished(j, TILE_CX0, N_CONV_TILES))),
        ],
        out_specs=[pl.BlockSpec((tm, QKV_OUT_TILE), lambda i, j: (i, finished(j, 0, N_QKV_TILES))),
                   pl.BlockSpec((tm, PROJ_TILE),
                                lambda i, j: (i, finished(j, TILE_CX0, N_CONV_TILES)))],
        out_shape=[jax.ShapeDtypeStruct((s, QKV_OUT_WIDTH), BF16),
                   jax.ShapeDtypeStruct((s, CONV_WIDTH), BF16)],
        scratch_shapes=[pltpu.VMEM((tm, D_MODEL), BF16),
                        pltpu.VMEM((N_CONV_TILES, CARRY_ROWS, PROJ_TILE), F32),
                        pltpu.VMEM((2, tm, PROJ_TILE), F32),
                        pltpu.VMEM((2 * N_CONV_TILES, tm, PROJ_TILE), F32)],
        compiler_params=_params("arbitrary", "arbitrary"),
        name="in_proj",
    )(x, g_mix, w_in, table, w_conv)


def _attn_kernel(lam_ref, gsub_ref, q0_ref, q1_ref, k0_ref, k1_ref, v_ref, o_ref,
                 m_ref, l_ref, acc_ref, *, t, lambda_init, side_work):
    i = pl.program_id(1)
    side_work()
    m_ref[...] = jnp.full(m_ref.shape, MASK_VALUE, F32)
    l_ref[...] = jnp.zeros(l_ref.shape, F32)
    acc_ref[...] = jnp.zeros(acc_ref.shape, F32)

    q = (q0_ref[...], q1_ref[...])
    k_refs = (k0_ref, k1_ref)
    n_chunks = t // LANES

    def tile(j, masked):
        start = pl.multiple_of(j * t, t)
        v = v_ref[pl.ds(start, t), :]
        if masked:
            keep = (lax.broadcasted_iota(jnp.int32, (t, t), 1)
                    <= lax.broadcasted_iota(jnp.int32, (t, t), 0))
        for c in range(2):
            k = k_refs[c][pl.ds(start, t), :]
            s = lax.dot_general(q[c], k, NT_DIMS, preferred_element_type=F32)
            if masked:
                s = jnp.where(keep, s, MASK_VALUE)
            chunks = [s[:, a * LANES:(a + 1) * LANES] for a in range(n_chunks)]
            m_prev = m_ref[c]
            m_new = jnp.maximum(m_prev, jnp.max(functools.reduce(jnp.maximum, chunks),
                                                axis=1, keepdims=True))
            alpha = jnp.exp(m_prev - m_new)
            ps = [jnp.exp(ch - m_new) for ch in chunks]
            l_ref[c] = alpha * l_ref[c] + functools.reduce(jnp.add, ps)
            p = jnp.concatenate([x.astype(BF16) for x in ps], axis=1)
            acc_ref[c] = alpha * acc_ref[c] + jnp.dot(p, v, preferred_element_type=F32)
            m_ref[c] = m_new

    def body(j, carry):
        tile(j, False)
        return carry

    lax.fori_loop(0, i, body, 0)
    tile(i, True)

    l0 = jnp.sum(l_ref[0], axis=1, keepdims=True)
    l1 = jnp.sum(l_ref[1], axis=1, keepdims=True)
    _attn_finish(acc_ref[0], l0, acc_ref[1], l1, lam_ref, gsub_ref, o_ref, lambda_init)


def _attn_finish(acc0, l0, acc1, l1, lam_ref, gsub_ref, o_ref, lambda_init):
    lam_rows = lam_ref[...]
    lam = (jnp.exp(jnp.sum(lam_rows[0:1] * lam_rows[1:2], axis=-1, keepdims=True))
           - jnp.exp(jnp.sum(lam_rows[2:3] * lam_rows[3:4], axis=-1, keepdims=True))
           + lambda_init)
    o = acc0 / l0 - lam * (acc1 / l1)
    o_ref[...] = (_rms_rows(o, gsub_ref[...]) * (1.0 - lambda_init)).astype(BF16)


def _attn_bounded_kernel(lam_ref, gsub_ref, q0_ref, q1_ref, k0_ref, k1_ref, v_ref, o_ref, acc_ref,
                         *, t, lambda_init, wide, side_work):
    i = pl.program_id(1)
    q = (q0_ref[...], q1_ref[...])
    k_refs = (k0_ref, k1_ref)

    def tile(first, n_tiles, diagonal_last):
        start = pl.multiple_of(first * t, t)
        n_keys = n_tiles * t
        if not diagonal_last:
            v = v_ref[pl.ds(start, n_keys), :]
            for c in range(2):
                k = k_refs[c][pl.ds(start, n_keys), :]
                s = lax.dot_general(q[c], k, NT_DIMS, preferred_element_type=F32)
                acc_ref[c] += jnp.dot(jnp.exp(s).astype(BF16), v, preferred_element_type=F32)
            return
        half = t // 2
        keep = (lax.broadcasted_iota(jnp.int32, (half, half), 1)
                <= lax.broadcasted_iota(jnp.int32, (half, half), 0))
        for c in range(2):
            for rows, n_seen in ((slice(0, half), n_keys - half), (slice(half, t), n_keys)):
                k = k_refs[c][pl.ds(start, n_seen), :]
                s = lax.dot_general(q[c][rows], k, NT_DIMS, preferred_element_type=F32)
                parts = [jnp.where(keep, s[:, n_seen - half:], MASK_VALUE)]
                if n_seen > half:
                    parts.insert(0, s[:, :n_seen - half])
                p = jnp.exp(jnp.concatenate(parts, axis=1)).astype(BF16)
                acc_ref[c, rows] = jnp.dot(p, v_ref[pl.ds(start, n_seen), :],
                                           preferred_element_type=F32)

    n_wide = i // wide
    for r in range(wide):
        @pl.when(i - n_wide * wide == r)
        def _(r=r):
            side_work()
            tile(i - r, r + 1, True)

    def wide_body(j, carry):
        tile(j * wide, wide, False)
        return carry

    lax.fori_loop(0, n_wide, wide_body, 0)

    acc0 = acc_ref[0]
    acc1 = acc_ref[1]
    _attn_finish(acc0[:, :ATT_V_DIM], acc0[:, ATT_V_DIM:],
                 acc1[:, :ATT_V_DIM], acc1[:, ATT_V_DIM:],
                 lam_ref, gsub_ref, o_ref, lambda_init)


ATT_WIDE_TILES = 4


BF16_ROWS = 16
N_ATTN_INPUTS = 7


def _with_weight_casts(attn_body, n):
    def kernel(*refs):
        ins, refs = refs[:N_ATTN_INPUTS], refs[N_ATTN_INPUTS:]
        cast_in, o_ref, cast_out, scratch = refs[:n], refs[n], refs[n + 1:2 * n + 1], refs[2 * n + 1:]

        def cast_weights():
            for src, dst in zip(cast_in, cast_out):
                dst[...] = src[...].astype(BF16)

        attn_body(*ins, o_ref, *scratch, side_work=cast_weights)

    return kernel


def _attn_call(lam_rows, g_sub, qkv, *cast_arrays, cast_jobs, t, lambda_init, bounded):
    s = qkv.shape[0]
    n_q = s // t
    n_steps = ATT_HEADS * n_q
    k_off = 2 * ATT_HEADS
    v_off = 2 * ATT_QK_WIDTH // ATT_HEAD_DIM
    if bounded:
        body = functools.partial(_attn_bounded_kernel, wide=ATT_WIDE_TILES)
        v_spec = pl.BlockSpec((s, V_GROUP), lambda h, i: (0, v_off * LANES // V_GROUP + h))
        scratch = [pltpu.VMEM((2, t, V_GROUP), F32)]
    else:
        body = _attn_kernel
        v_spec = pl.BlockSpec((s, ATT_V_DIM), lambda h, i: (0, v_off + (V_GROUP // ATT_V_DIM) * h))
        scratch = [pltpu.VMEM((2, t, LANES), F32),
                   pltpu.VMEM((2, t, LANES), F32),
                   pltpu.VMEM((2, t, ATT_V_DIM), F32)]
    cast_in_specs, cast_out_specs, cast_out_shapes = [], [], []
    for arr, (width, col_block) in zip(cast_arrays, cast_jobs):
        rows = arr.shape[0]
        col_blocks = 1
        while rows % (BF16_ROWS * n_steps // col_blocks):
            col_blocks *= 2
        block = (rows * col_blocks // n_steps, width // col_blocks)
        assert block[1] % LANES == 0 and block[1] * col_blocks == width
        cast_in_specs.append(pl.BlockSpec(
            block, lambda h, i, cb=col_blocks, c0=col_block * col_blocks:
            ((h * n_q + i) // cb, c0 + (h * n_q + i) % cb)))
        cast_out_specs.append(pl.BlockSpec(
            block, lambda h, i, cb=col_blocks: ((h * n_q + i) // cb, (h * n_q + i) % cb)))
        cast_out_shapes.append(jax.ShapeDtypeStruct((rows, width), BF16))
    return pl.pallas_call(
        _with_weight_casts(functools.partial(body, t=t, lambda_init=lambda_init), len(cast_jobs)),
        grid=(ATT_HEADS, n_q),
        in_specs=[
            pl.BlockSpec((4, ATT_HEAD_DIM), lambda h, i: (0, 0)),
            pl.BlockSpec((1, ATT_V_DIM), lambda h, i: (0, 0)),
            pl.BlockSpec((t, LANES), lambda h, i: (i, 2 * h)),
            pl.BlockSpec((t, LANES), lambda h, i: (i, 2 * h + 1)),
            pl.BlockSpec((s, LANES), lambda h, i: (0, k_off + 2 * h)),
            pl.BlockSpec((s, LANES), lambda h, i: (0, k_off + 2 * h + 1)),
            v_spec,
        ] + cast_in_specs,
        out_specs=[pl.BlockSpec((t, ATT_V_DIM), lambda h, i: (i, h))] + cast_out_specs,
        out_shape=[jax.ShapeDtypeStruct((s, ATT_WIDTH), BF16)] + cast_out_shapes,
        scratch_shapes=scratch,
        compiler_params=_params("arbitrary", "arbitrary"),
        name="diff_attn_bounded" if bounded else "diff_attn",
    )(lam_rows, g_sub, qkv, qkv, qkv, qkv, qkv, *cast_arrays)


SCORE_BOUND_LIMIT = 40.0


def _attention(lam_rows, g_sub, g_q, g_k, qkv, cast_arrays, cast_jobs, *, t, lambda_init):
    bound = (1.01 * ATT_HEAD_DIM ** 0.5
             * jnp.max(jnp.abs(g_q.astype(F32))) * jnp.max(jnp.abs(g_k.astype(F32))))
    call = functools.partial(_attn_call, cast_jobs=cast_jobs, t=t, lambda_init=lambda_init)
    return lax.cond(bound <= SCORE_BOUND_LIMIT,
                    functools.partial(call, bounded=True),
                    functools.partial(call, bounded=False),
                    lam_rows, g_sub, qkv, *cast_arrays)


def _merge_kernel(x_ref, g_ref, ya_ref, yc_ref, wga_ref, wgc_ref, bg_ref, wa_ref, wc_ref, wo_ref,
                  gffn_ref, x1_ref, h2_ref):
    xf = x_ref[...]
    h = _rms_rows(xf, g_ref[...]).astype(BF16)
    bg = bg_ref[...]
    gate_a = jax.nn.sigmoid(jnp.dot(h, wga_ref[...], preferred_element_type=F32) + bg[0:1])
    merged = gate_a * jnp.dot(ya_ref[...], wa_ref[...], preferred_element_type=F32)
    gate_c = jax.nn.sigmoid(jnp.dot(h, wgc_ref[...], preferred_element_type=F32) + bg[1:2])
    merged = merged + gate_c * jnp.dot(yc_ref[...], wc_ref[...], preferred_element_type=F32)
    x1 = xf + jnp.dot(merged.astype(BF16), wo_ref[...], preferred_element_type=F32)
    x1_ref[...] = x1
    h2_ref[...] = _rms_rows(x1, gffn_ref[...]).astype(BF16)


def _merge_call(x, g_mix, y_att, y_conv, w_ga, w_gc, b_gates, w_att_out, w_conv_out, w_o, g_ffn,
                *, tm):
    s = x.shape[0]
    const = lambda i: (0, 0)
    return pl.pallas_call(
        _merge_kernel,
        grid=(s // tm,),
        in_specs=[
            pl.BlockSpec((tm, D_MODEL), lambda i: (i, 0)),
            _resident((1, D_MODEL), const),
            pl.BlockSpec((tm, ATT_WIDTH), lambda i: (i, 0)),
            pl.BlockSpec((tm, CONV_WIDTH), lambda i: (i, 0)),
            _resident((D_MODEL, D_MODEL), const),
            _resident((D_MODEL, D_MODEL), const),
            _resident((2, D_MODEL), const),
            _resident((ATT_WIDTH, D_MODEL), const),
            _resident((CONV_WIDTH, D_MODEL), const),
            _resident((D_MODEL, D_MODEL), const),
            _resident((1, D_MODEL), const),
        ],
        out_specs=[pl.BlockSpec((tm, D_MODEL), lambda i: (i, 0)),
                   pl.BlockSpec((tm, D_MODEL), lambda i: (i, 0))],
        out_shape=[jax.ShapeDtypeStruct((s, D_MODEL), F32),
                   jax.ShapeDtypeStruct((s, D_MODEL), BF16)],
        compiler_params=_params("arbitrary"),
        name="merge",
    )(x, g_mix, y_att, y_conv, w_ga, w_gc, b_gates, w_att_out, w_conv_out, w_o, g_ffn)


FFN_SLICE = 256


def _ffn_kernel(h_ref, wg_ref, wv_ref, cwg_ref, cwv_ref, wd_ref, o_ref,
                carry_ref, pre_ref, act_ref, *, tm):
    i = pl.program_id(0)
    f = pl.program_id(1)

    @pl.when(i == 0)
    def _():
        carry_ref[f] = jnp.zeros(carry_ref.shape[1:], F32)

    @pl.when(f == 0)
    def _():
        o_ref[...] = jnp.zeros(o_ref.shape, F32)

    h = h_ref[...]
    n_slices = wg_ref.shape[1] // FFN_SLICE
    slices = [slice(a * FFN_SLICE, (a + 1) * FFN_SLICE) for a in range(n_slices)]
    for a, sl in enumerate(slices):
        for b, (w_ref, cw_ref) in enumerate(((wg_ref, cwg_ref), (wv_ref, cwv_ref))):
            pre_ref[b, a, 0:CARRY_ROWS] = carry_ref[f, b, :, sl]
            pre_ref[b, a, CARRY_ROWS:CARRY_ROWS + tm] = jnp.dot(h, w_ref[:, sl],
                                                                preferred_element_type=F32)
            carry_ref[f, b, :, sl] = pre_ref[b, a, tm:tm + CARRY_ROWS]
    for a, sl in enumerate(slices):
        u = []
        for b, cw_ref in enumerate((cwg_ref, cwv_ref)):
            w = cw_ref[:, sl]
            u.append(w[0:1] * pre_ref[b, a, CARRY_ROWS - 2:CARRY_ROWS - 2 + tm]
                     + w[1:2] * pre_ref[b, a, CARRY_ROWS - 1:CARRY_ROWS - 1 + tm]
                     + w[2:3] * pre_ref[b, a, CARRY_ROWS:CARRY_ROWS + tm])
        act_ref[a] = (u[0] * jax.nn.sigmoid(u[0]) * u[1]).astype(BF16)
    for a, sl in enumerate(slices):
        for n in range(o_ref.shape[1] // FFN_SLICE):
            cs = slice(n * FFN_SLICE, (n + 1) * FFN_SLICE)
            o_ref[:, cs] += jnp.dot(act_ref[a], wd_ref[sl, cs], preferred_element_type=F32)


def _ffn_call(h2, w_up, w_ffn_conv, w_down, *, tm, tf):
    s = h2.shape[0]
    n_f = D_FF // tf
    return pl.pallas_call(
        functools.partial(_ffn_kernel, tm=tm),
        grid=(s // tm, n_f),
        in_specs=[
            pl.BlockSpec((tm, D_MODEL), lambda i, f: (i, 0)),
            pl.BlockSpec((D_MODEL, tf), lambda i, f: (0, f)),
            pl.BlockSpec((D_MODEL, tf), lambda i, f: (0, n_f + f)),
            pl.BlockSpec((3, tf), lambda i, f: (0, f)),
            pl.BlockSpec((3, tf), lambda i, f: (0, n_f + f)),
            pl.BlockSpec((tf, D_MODEL), lambda i, f: (f, 0)),
        ],
        out_specs=pl.BlockSpec((tm, D_MODEL), lambda i, f: (i, 0)),
        out_shape=jax.ShapeDtypeStruct((s, D_MODEL), F32),
        scratch_shapes=[pltpu.VMEM((n_f, 2, CARRY_ROWS, tf), F32),
                        pltpu.VMEM((2, tf // FFN_SLICE, CARRY_ROWS + tm, FFN_SLICE), F32),
                        pltpu.VMEM((tf // FFN_SLICE, tm, FFN_SLICE), BF16)],
        compiler_params=_params("arbitrary", "arbitrary"),
        name="conv_ffn",
    )(h2, w_up, w_up, w_ffn_conv, w_ffn_conv, w_down)


def _ple_kernel(x_ref, y_ref, p_ref, wple_ref, gple_ref, gpg_ref, wpg_ref, bpg_ref, o_ref):
    xf = x_ref[...] + y_ref[...]
    pe = _rms_rows(jnp.dot(p_ref[...].astype(BF16), wple_ref[...], preferred_element_type=F32),
                   gple_ref[...])
    hn = _rms_rows(xf, gpg_ref[...]).astype(BF16)
    pg = jax.nn.sigmoid(jnp.dot(hn, wpg_ref[...], preferred_element_type=F32) + bpg_ref[...])
    o_ref[...] = xf + pg * pe


def _ple_call(x1, y_ffn, p, w_ple, g_ple, g_pg, w_pg, b_pg, *, tm):
    s = x1.shape[0]
    const = lambda i: (0, 0)
    return pl.pallas_call(
        _ple_kernel,
        grid=(s // tm,),
        in_specs=[
            pl.BlockSpec((tm, D_MODEL), lambda i: (i, 0)),
            pl.BlockSpec((tm, D_MODEL), lambda i: (i, 0)),
            pl.BlockSpec((tm, PLE_DIM), lambda i: (i, 0)),
            _resident((PLE_DIM, D_MODEL), const),
            _resident((1, D_MODEL), const),
            _resident((1, D_MODEL), const),
            _resident((D_MODEL, D_MODEL), const),
            _resident((1, D_MODEL), const),
        ],
        out_specs=pl.BlockSpec((tm, D_MODEL), lambda i: (i, 0)),
        out_shape=jax.ShapeDtypeStruct((s, D_MODEL), F32),
        compiler_params=_params("arbitrary"),
        name="ple_gate",
    )(x1, y_ffn, p, w_ple, g_ple, g_pg, w_pg, b_pg)


def _layer(depth_index, x, p, g_mix, w_in, b_gates, g_q, g_k, lam_rows, g_sub, w_conv_mix, w_att_out,
           w_conv_out, w_o, g_ffn, w_up, w_ffn_conv, w_down, w_ple, g_ple, g_pg, w_pg, b_pg):
    lambda_init = 0.8 - 0.6 * math.exp(-0.3 * depth_index)
    row = lambda a: a.reshape(1, -1).astype(F32)

    g_mix = row(g_mix)
    qkv, y_conv = _inproj_call(x, g_mix, w_in.astype(F32), _qk_table(g_q, g_k),
                               w_conv_mix.astype(F32), tm=1024)
    later = [w_in, w_in, w_att_out, w_conv_out, w_o, w_up, w_down, w_ple, w_pg]
    jobs = ([(D_MODEL, OFF_GA // D_MODEL), (D_MODEL, OFF_GC // D_MODEL)]
            + [(w.shape[1], 0) for w in later[2:]])
    y_att, w_ga, w_gc, w_att_out, w_conv_out, w_o, w_up, w_down, w_ple, w_pg = _attention(
        lam_rows, row(g_sub), g_q, g_k, qkv, [w.astype(F32) for w in later], jobs,
        t=1024, lambda_init=lambda_init)
    x1, h2 = _merge_call(x, g_mix, y_att, y_conv, w_ga, w_gc, b_gates.astype(F32),
                         w_att_out, w_conv_out, w_o, row(g_ffn), tm=256)
    y_ffn = _ffn_call(h2, w_up, w_ffn_conv.astype(F32), w_down, tm=1024, tf=512)
    return _ple_call(x1, y_ffn, p, w_ple, row(g_ple), row(g_pg), w_pg, row(b_pg), tm=512)


def kernel(x, p, g_mix, w_in, b_gates, g_q, g_k, lam_q1, lam_k1, lam_q2, lam_k2, g_sub, w_conv_mix,
           w_att_out, w_conv_out, w_o, g_ffn, w_up, w_ffn_conv, w_down, w_ple, g_ple, g_pg, w_pg,
           b_pg):
    batch, seq, d_model = x.shape
    depth = p.shape[0]
    outs = []
    for b in range(batch):
        xb = x.reshape(seq, d_model) if batch == 1 else x[b]
        for i in range(depth):
            lam_rows = jnp.stack([lam_q1[i], lam_k1[i], lam_q2[i], lam_k2[i]]).astype(F32)
            pb = p.reshape(seq, PLE_DIM) if batch == 1 and depth == 1 else p[i, b]
            xb = _layer(i, xb, pb, g_mix[i], w_in[i], b_gates[i], g_q[i], g_k[i], lam_rows,
                        g_sub[i], w_conv_mix[i], w_att_out[i], w_conv_out[i], w_o[i], g_ffn[i],
                        w_up[i], w_ffn_conv[i], w_down[i], w_ple[i], g_ple[i], g_pg[i], w_pg[i],
                        b_pg[i])
        outs.append(xb)
    return outs[0].reshape(x.shape) if batch == 1 else jnp.stack(outs)
```

```python
import functools
import math

import jax
import jax.numpy as jnp
import numpy as np
from jax import lax
from jax.experimental import pallas as pl
from jax.experimental.pallas import tpu as pltpu

F32 = jnp.float32
BF16 = jnp.bfloat16

D_MODEL = 2048
ATT_HEADS = 8
ATT_HEAD_DIM = 64
ATT_V_DIM = 2 * ATT_HEAD_DIM
ATT_QK_WIDTH = ATT_HEADS * 2 * ATT_HEAD_DIM
ATT_WIDTH = ATT_HEADS * ATT_V_DIM
CONV_WIDTH = 1024
D_FF = 5632
PLE_DIM = 256
EPS = 1e-6

LANES = 128
CARRY_ROWS = 8
MASK_VALUE = -1e30
VMEM_LIMIT = 56 * 1024 * 1024

OFF_GA = 2 * ATT_QK_WIDTH + ATT_WIDTH + 3 * CONV_WIDTH
OFF_GC = OFF_GA + D_MODEL

NT_DIMS = (((1,), (1,)), ((), ()))


def _rms_rows(xf, g):
    ms = jnp.mean(xf * xf, axis=-1, keepdims=True)
    return xf * lax.rsqrt(ms + EPS) * g


def _params(*sem):
    return pltpu.CompilerParams(dimension_semantics=sem, vmem_limit_bytes=VMEM_LIMIT)


def _resident(shape, index_map):
    return pl.BlockSpec(shape, index_map, pipeline_mode=pl.Buffered(1))


def _shift_rows(u, prev, rows):
    u1 = jnp.where(rows == 0, prev[CARRY_ROWS - 1:CARRY_ROWS], pltpu.roll(u, 1, 0))
    u2 = jnp.where(rows == 0, prev[CARRY_ROWS - 2:CARRY_ROWS - 1],
                   jnp.where(rows == 1, prev[CARRY_ROWS - 1:CARRY_ROWS], pltpu.roll(u, 2, 0)))
    return u1, u2


def _causal_conv3(u, w, prev):
    rows = lax.broadcasted_iota(jnp.int32, u.shape, 0)
    u1, u2 = _shift_rows(u, prev, rows)
    return w[0:1] * u2 + w[1:2] * u1 + w[2:3] * u


ROW_GAIN, ROW_ONES, ROW_SLOPE, ROW_HI, ROW_MID, ROW_LO, TABLE_ROWS = 0, 1, 2, 3, 4, 5, 8
V_GROUP = 2 * LANES
PROJ_TILE = 512
N_QK_TILES = 2 * ATT_QK_WIDTH // PROJ_TILE
N_QKV_TILES = N_QK_TILES + ATT_WIDTH // PROJ_TILE
N_CONV_TILES = CONV_WIDTH // PROJ_TILE
N_PROJ_TILES = N_QKV_TILES + 3 * N_CONV_TILES
TILE_V0 = N_QK_TILES
TILE_CB0 = N_QKV_TILES
TILE_CX0 = N_QKV_TILES + 2 * N_CONV_TILES
QKV_OUT_TILE = 2 * PROJ_TILE
QKV_OUT_WIDTH = N_QKV_TILES * QKV_OUT_TILE


def _qk_tile(acc, tab_ref, o_ref, first_pos):
    tm = acc.shape[0]
    lane = lax.broadcasted_iota(jnp.int32, (tm, LANES), 1)
    pos = (first_pos + lax.broadcasted_iota(jnp.int32, (tm, LANES), 0)).astype(F32)
    first = lane < ATT_HEAD_DIM
    for c in range(acc.shape[1] // LANES):
        t = acc[:, c * LANES:(c + 1) * LANES]
        sl = slice(c * LANES, (c + 1) * LANES)
        sq = t * t
        ss0 = jnp.sum(jnp.where(first, sq, 0.0), axis=-1, keepdims=True)
        ss1 = jnp.sum(jnp.where(first, 0.0, sq), axis=-1, keepdims=True)
        r = jnp.where(first, lax.rsqrt(ss0 / ATT_HEAD_DIM + EPS), lax.rsqrt(ss1 / ATT_HEAD_DIM + EPS))
        tnorm = t * r * tab_ref[ROW_GAIN:ROW_GAIN + 1, sl]
        bias = tab_ref[ROW_SLOPE:ROW_SLOPE + 1, sl] * pos
        hi = bias.astype(BF16).astype(F32)
        rem = bias - hi
        mid = rem.astype(BF16).astype(F32)
        lo = rem - mid
        aug = (tab_ref[ROW_ONES:ROW_ONES + 1, sl] + tab_ref[ROW_HI:ROW_HI + 1, sl] * hi
               + tab_ref[ROW_MID:ROW_MID + 1, sl] * mid + tab_ref[ROW_LO:ROW_LO + 1, sl] * lo)
        comp0 = jnp.where(first, tnorm, aug)
        comp1 = jnp.where(first, pltpu.roll(tnorm, ATT_HEAD_DIM, 1), aug)
        out0 = 2 * c * LANES
        o_ref[:, out0:out0 + LANES] = comp0.astype(BF16)
        o_ref[:, out0 + LANES:out0 + 2 * LANES] = comp1.astype(BF16)


def _v_tile(acc, o_ref):
    tm = acc.shape[0]
    ones = jnp.ones((tm, V_GROUP - ATT_V_DIM), BF16)
    for a in range(PROJ_TILE // ATT_V_DIM):
        o_ref[:, a * V_GROUP:a * V_GROUP + ATT_V_DIM] = acc[:, a * ATT_V_DIM:(a + 1) * ATT_V_DIM].astype(BF16)
        o_ref[:, a * V_GROUP + ATT_V_DIM:(a + 1) * V_GROUP] = ones


def _inproj_kernel(x_ref, g_ref, w_ref, tab_ref, cw_ref, qkv_ref, conv_ref,
                   h_ref, carry_ref, park_ref, hold_ref, *, tm):
    i = pl.program_id(0)
    j = pl.program_id(1)

    def project(dst_ref, slot):
        dst_ref[slot] = jnp.dot(h_ref[...], w_ref[...].astype(BF16), preferred_element_type=F32)

    def finish_conv(jc, cx_slot):
        z = hold_ref[N_CONV_TILES + jc] * park_ref[cx_slot]
        y = _causal_conv3(z, cw_ref[...], carry_ref[jc])
        conv_ref[...] = (hold_ref[jc] * y).astype(BF16)
        carry_ref[jc] = z[tm - CARRY_ROWS:tm]

    @pl.when(j == 0)
    def _():
        @pl.when(i == 0)
        def _():
            carry_ref[...] = jnp.zeros(carry_ref.shape, F32)

        h_ref[...] = _rms_rows(x_ref[...], g_ref[...]).astype(BF16)
        project(park_ref, 0)

    @pl.when(jnp.logical_and(j >= 1, j <= TILE_V0))
    def _():
        _qk_tile(park_ref[(j - 1) % 2], tab_ref, qkv_ref, i * tm)
        project(park_ref, j % 2)

    @pl.when(j == TILE_V0 + 1)
    def _():
        _v_tile(park_ref[TILE_V0 % 2], qkv_ref)
        project(park_ref, (TILE_V0 + 1) % 2)

    @pl.when(j == TILE_CB0)
    def _():
        _v_tile(park_ref[(TILE_V0 + 1) % 2], qkv_ref)
        project(hold_ref, 0)

    @pl.when(jnp.logical_and(j > TILE_CB0, j < TILE_CX0))
    def _():
        project(hold_ref, j - TILE_CB0)

    @pl.when(j == TILE_CX0)
    def _():
        project(park_ref, 0)

    @pl.when(j == TILE_CX0 + 1)
    def _():
        finish_conv(0, 0)
        project(park_ref, 1)

    @pl.when(j == TILE_CX0 + 2)
    def _():
        finish_conv(1, 1)


def _qk_table(g_q, g_k):
    n_groups = ATT_QK_WIDTH // ATT_HEAD_DIM
    slopes = jnp.asarray(2.0 ** (-8.0 * np.arange(1, ATT_HEADS + 1) / ATT_HEADS), F32)
    slope_cols = jnp.repeat(slopes, 2 * ATT_HEAD_DIM)
    lane = jnp.tile(jnp.arange(LANES), ATT_HEADS) - ATT_HEAD_DIM
    at = lambda n: (lane == n).astype(F32)
    zeros = jnp.zeros(ATT_QK_WIDTH, F32)
    both = lambda q_part, k_part: jnp.concatenate([q_part, k_part])
    rows = [None] * TABLE_ROWS
    rows[ROW_GAIN] = both(jnp.tile(g_q.astype(F32) * ATT_HEAD_DIM ** -0.5, n_groups),
                          jnp.tile(g_k.astype(F32), n_groups))
    rows[ROW_ONES] = both(at(0) + at(1) + at(2), at(3) + at(4) + at(5))
    rows[ROW_SLOPE] = both(-slope_cols, slope_cols)
    rows[ROW_HI] = both(at(3), at(0))
    rows[ROW_MID] = both(at(4), at(1))
    rows[ROW_LO] = both(at(5), at(2))
    return jnp.stack([both(zeros, zeros) if r is None else r for r in rows])


def _inproj_call(x, g_mix, w_in, table, w_conv, *, tm):
    s = x.shape[0]
    finished = lambda j, first, count: jnp.clip(j - 1 - first, 0, count - 1)
    return pl.pallas_call(
        functools.partial(_inproj_kernel, tm=tm),
        grid=(s // tm, N_PROJ_TILES + 1),
        in_specs=[
            pl.BlockSpec((tm, D_MODEL), lambda i, j: (i, 0)),
            pl.BlockSpec((1, D_MODEL), lambda i, j: (0, 0)),
            pl.BlockSpec((D_MODEL, PROJ_TILE), lambda i, j: (0, jnp.minimum(j, N_PROJ_TILES - 1))),
            pl.BlockSpec((TABLE_ROWS, PROJ_TILE), lambda i, j: (0, finished(j, 0, N_QK_TILES))),
            pl.BlockSpec((3, PROJ_TILE), lambda i, j: (0, finished(j, TILE_CX0, N_CONV_TILES))),
        ],
        out_specs=[pl.BlockSpec((tm, QKV_OUT_TILE), lambda i, j: (i, finished(j, 0, N_QKV_TILES))),
                   pl.BlockSpec((tm, PROJ_TILE),
                                lambda i, j: (i, finished(j, TILE_CX0, N_CONV_TILES)))],
        out_shape=[jax.ShapeDtypeStruct((s, QKV_OUT_WIDTH), BF16),
                   jax.ShapeDtypeStruct((s, CONV_WIDTH), BF16)],
        scratch_shapes=[pltpu.VMEM((tm, D_MODEL), BF16),
                        pltpu.VMEM((N_CONV_TILES, CARRY_ROWS, PROJ_TILE), F32),
                        pltpu.VMEM((2, tm, PROJ_TILE), F32),
                        pltpu.VMEM((2 * N_CONV_TILES, tm, PROJ_TILE), F32)],
        compiler_params=_params("arbitrary", "arbitrary"),
        name="in_proj",
    )(x, g_mix, w_in, table, w_conv)


def _attn_kernel(lam_ref, gsub_ref, q0_ref, q1_ref, k0_ref, k1_ref, v_ref, o_ref,
                 m_ref, l_ref, acc_ref, *, t, lambda_init, side_work):
    i = pl.program_id(1)
    side_work()
    m_ref[...] = jnp.full(m_ref.shape, MASK_VALUE, F32)
    l_ref[...] = jnp.zeros(l_ref.shape, F32)
    acc_ref[...] = jnp.zeros(acc_ref.shape, F32)

    q = (q0_ref[...], q1_ref[...])
    k_refs = (k0_ref, k1_ref)
    n_chunks = t // LANES

    def tile(j, masked):
        start = pl.multiple_of(j * t, t)
        v = v_ref[pl.ds(start, t), :]
        if masked:
            keep = (lax.broadcasted_iota(jnp.int32, (t, t), 1)
                    <= lax.broadcasted_iota(jnp.int32, (t, t), 0))
        for c in range(2):
            k = k_refs[c][pl.ds(start, t), :]
            s = lax.dot_general(q[c], k, NT_DIMS, preferred_element_type=F32)
            if masked:
                s = jnp.where(keep, s, MASK_VALUE)
            chunks = [s[:, a * LANES:(a + 1) * LANES] for a in range(n_chunks)]
            m_prev = m_ref[c]
            m_new = jnp.maximum(m_prev, jnp.max(functools.reduce(jnp.maximum, chunks),
                                                axis=1, keepdims=True))
            alpha = jnp.exp(m_prev - m_new)
            ps = [jnp.exp(ch - m_new) for ch in chunks]
            l_ref[c] = alpha * l_ref[c] + functools.reduce(jnp.add, ps)
            p = jnp.concatenate([x.astype(BF16) for x in ps], axis=1)
            acc_ref[c] = alpha * acc_ref[c] + jnp.dot(p, v, preferred_element_type=F32)
            m_ref[c] = m_new

    def body(j, carry):
        tile(j, False)
        return carry

    lax.fori_loop(0, i, body, 0)
    tile(i, True)

    l0 = jnp.sum(l_ref[0], axis=1, keepdims=True)
    l1 = jnp.sum(l_ref[1], axis=1, keepdims=True)
    _attn_finish(acc_ref[0], l0, acc_ref[1], l1, lam_ref, gsub_ref, o_ref, lambda_init)


def _attn_finish(acc0, l0, acc1, l1, lam_ref, gsub_ref, o_ref, lambda_init):
    lam_rows = lam_ref[...]
    lam = (jnp.exp(jnp.sum(lam_rows[0:1] * lam_rows[1:2], axis=-1, keepdims=True))
           - jnp.exp(jnp.sum(lam_rows[2:3] * lam_rows[3:4], axis=-1, keepdims=True))
           + lambda_init)
    o = acc0 / l0 - lam * (acc1 / l1)
    o_ref[...] = (_rms_rows(o, gsub_ref[...]) * (1.0 - lambda_init)).astype(BF16)


def _attn_bounded_kernel(lam_ref, gsub_ref, q0_ref, q1_ref, k0_ref, k1_ref, v_ref, o_ref, acc_ref,
                         *, t, lambda_init, wide, side_work):
    i = pl.program_id(1)
    q = (q0_ref[...], q1_ref[...])
    k_refs = (k0_ref, k1_ref)

    def tile(first, n_tiles, diagonal_last):
        start = pl.multiple_of(first * t, t)
        n_keys = n_tiles * t
        if not diagonal_last:
            v = v_ref[pl.ds(start, n_keys), :]
            for c in range(2):
                k = k_refs[c][pl.ds(start, n_keys), :]
                s = lax.dot_general(q[c], k, NT_DIMS, preferred_element_type=F32)
                acc_ref[c] += jnp.dot(jnp.exp(s).astype(BF16), v, preferred_element_type=F32)
            return
        half = t // 2
        keep = (lax.broadcasted_iota(jnp.int32, (half, half), 1)
                <= lax.broadcasted_iota(jnp.int32, (half, half), 0))
        for c in range(2):
            for rows, n_seen in ((slice(0, half), n_keys - half), (slice(half, t), n_keys)):
                k = k_refs[c][pl.ds(start, n_seen), :]
                s = lax.dot_general(q[c][rows], k, NT_DIMS, preferred_element_type=F32)
                parts = [jnp.where(keep, s[:, n_seen - half:], MASK_VALUE)]
                if n_seen > half:
                    parts.insert(0, s[:, :n_seen - half])
                p = jnp.exp(jnp.concatenate(parts, axis=1)).astype(BF16)
                acc_ref[c, rows] = jnp.dot(p, v_ref[pl.ds(start, n_seen), :],
                                           preferred_element_type=F32)

    n_wide = i // wide
    for r in range(wide):
        @pl.when(i - n_wide * wide == r)
        def _(r=r):
            side_work()
            tile(i - r, r + 1, True)

    def wide_body(j, carry):
        tile(j * wide, wide, False)
        return carry

    lax.fori_loop(0, n_wide, wide_body, 0)

    acc0 = acc_ref[0]
    acc1 = acc_ref[1]
    _attn_finish(acc0[:, :ATT_V_DIM], acc0[:, ATT_V_DIM:],
                 acc1[:, :ATT_V_DIM], acc1[:, ATT_V_DIM:],
                 lam_ref, gsub_ref, o_ref, lambda_init)


ATT_WIDE_TILES = 4


BF16_ROWS = 16
N_ATTN_INPUTS = 7


def _with_weight_casts(attn_body, n):
    def kernel(*refs):
        ins, refs = refs[:N_ATTN_INPUTS], refs[N_ATTN_INPUTS:]
        cast_in, o_ref, cast_out, scratch = refs[:n], refs[n], refs[n + 1:2 * n + 1], refs[2 * n + 1:]

        def cast_weights():
            for src, dst in zip(cast_in, cast_out):
                dst[...] = src[...].astype(BF16)

        attn_body(*ins, o_ref, *scratch, side_work=cast_weights)

    return kernel


def _attn_call(lam_rows, g_sub, qkv, *cast_arrays, cast_jobs, t, lambda_init, bounded):
    s = qkv.shape[0]
    n_q = s // t
    n_steps = ATT_HEADS * n_q
    k_off = 2 * ATT_HEADS
    v_off = 2 * ATT_QK_WIDTH // ATT_HEAD_DIM
    if bounded:
        body = functools.partial(_attn_bounded_kernel, wide=ATT_WIDE_TILES)
        v_spec = pl.BlockSpec((s, V_GROUP), lambda h, i: (0, v_off * LANES // V_GROUP + h))
        scratch = [pltpu.VMEM((2, t, V_GROUP), F32)]
    else:
        body = _attn_kernel
        v_spec = pl.BlockSpec((s, ATT_V_DIM), lambda h, i: (0, v_off + (V_GROUP // ATT_V_DIM) * h))
        scratch = [pltpu.VMEM((2, t, LANES), F32),
                   pltpu.VMEM((2, t, LANES), F32),
                   pltpu.VMEM((2, t, ATT_V_DIM), F32)]
    cast_in_specs, cast_out_specs, cast_out_shapes = [], [], []
    for arr, (width, col_block) in zip(cast_arrays, cast_jobs):
        rows = arr.shape[0]
        col_blocks = 1
        while rows % (BF16_ROWS * n_steps // col_blocks):
            col_blocks *= 2
        block = (rows * col_blocks // n_steps, width // col_blocks)
        assert block[1] % LANES == 0 and block[1] * col_blocks == width
        cast_in_specs.append(pl.BlockSpec(
            block, lambda h, i, cb=col_blocks, c0=col_block * col_blocks:
            ((h * n_q + i) // cb, c0 + (h * n_q + i) % cb)))
        cast_out_specs.append(pl.BlockSpec(
            block, lambda h, i, cb=col_blocks: ((h * n_q + i) // cb, (h * n_q + i) % cb)))
        cast_out_shapes.append(jax.ShapeDtypeStruct((rows, width), BF16))
    return pl.pallas_call(
        _with_weight_casts(functools.partial(body, t=t, lambda_init=lambda_init), len(cast_jobs)),
        grid=(ATT_HEADS, n_q),
        in_specs=[
            pl.BlockSpec((4, ATT_HEAD_DIM), lambda h, i: (0, 0)),
            pl.BlockSpec((1, ATT_V_DIM), lambda h, i: (0, 0)),
            pl.BlockSpec((t, LANES), lambda h, i: (i, 2 * h)),
            pl.BlockSpec((t, LANES), lambda h, i: (i, 2 * h + 1)),
            pl.BlockSpec((s, LANES), lambda h, i: (0, k_off + 2 * h)),
            pl.BlockSpec((s, LANES), lambda h, i: (0, k_off + 2 * h + 1)),
            v_spec,
        ] + cast_in_specs,
        out_specs=[pl.BlockSpec((t, ATT_V_DIM), lambda h, i: (i, h))] + cast_out_specs,
        out_shape=[jax.ShapeDtypeStruct((s, ATT_WIDTH), BF16)] + cast_out_shapes,
        scratch_shapes=scratch,
        compiler_params=_params("arbitrary", "arbitrary"),
        name="diff_attn_bounded" if bounded else "diff_attn",
    )(lam_rows, g_sub, qkv, qkv, qkv, qkv, qkv, *cast_arrays)


SCORE_BOUND_LIMIT = 40.0


def _attention(lam_rows, g_sub, g_q, g_k, qkv, cast_arrays, cast_jobs, *, t, lambda_init):
    bound = (1.01 * ATT_HEAD_DIM ** 0.5
             * jnp.max(jnp.abs(g_q.astype(F32))) * jnp.max(jnp.abs(g_k.astype(F32))))
    call = functools.partial(_attn_call, cast_jobs=cast_jobs, t=t, lambda_init=lambda_init)
    return lax.cond(bound <= SCORE_BOUND_LIMIT,
                    functools.partial(call, bounded=True),
                    functools.partial(call, bounded=False),
                    lam_rows, g_sub, qkv, *cast_arrays)


def _merge_kernel(x_ref, g_ref, ya_ref, yc_ref, wga_ref, wgc_ref, bg_ref, wa_ref, wc_ref, wo_ref,
                  gffn_ref, x1_ref, h2_ref):
    xf = x_ref[...]
    h = _rms_rows(xf, g_ref[...]).astype(BF16)
    bg = bg_ref[...]
    gate_a = jax.nn.sigmoid(jnp.dot(h, wga_ref[...], preferred_element_type=F32) + bg[0:1])
    merged = gate_a * jnp.dot(ya_ref[...], wa_ref[...], preferred_element_type=F32)
    gate_c = jax.nn.sigmoid(jnp.dot(h, wgc_ref[...], preferred_element_type=F32) + bg[1:2])
    merged = merged + gate_c * jnp.dot(yc_ref[...], wc_ref[...], preferred_element_type=F32)
    x1 = xf + jnp.dot(merged.astype(BF16), wo_ref[...], preferred_element_type=F32)
    x1_ref[...] = x1
    h2_ref[...] = _rms_rows(x1, gffn_ref[...]).astype(BF16)


def _merge_call(x, g_mix, y_att, y_conv, w_ga, w_gc, b_gates, w_att_out, w_conv_out, w_o, g_ffn,
                *, tm):
    s = x.shape[0]
    const = lambda i: (0, 0)
    return pl.pallas_call(
        _merge_kernel,
        grid=(s // tm,),
        in_specs=[
            pl.BlockSpec((tm, D_MODEL), lambda i: (i, 0)),
            _resident((1, D_MODEL), const),
            pl.BlockSpec((tm, ATT_WIDTH), lambda i: (i, 0)),
            pl.BlockSpec((tm, CONV_WIDTH), lambda i: (i, 0)),
            _resident((D_MODEL, D_MODEL), const),
            _resident((D_MODEL, D_MODEL), const),
            _resident((2, D_MODEL), const),
            _resident((ATT_WIDTH, D_MODEL), const),
            _resident((CONV_WIDTH, D_MODEL), const),
            _resident((D_MODEL, D_MODEL), const),
            _resident((1, D_MODEL), const),
        ],
        out_specs=[pl.BlockSpec((tm, D_MODEL), lambda i: (i, 0)),
                   pl.BlockSpec((tm, D_MODEL), lambda i: (i, 0))],
        out_shape=[jax.ShapeDtypeStruct((s, D_MODEL), F32),
                   jax.ShapeDtypeStruct((s, D_MODEL), BF16)],
        compiler_params=_params("arbitrary"),
        name="merge",
    )(x, g_mix, y_att, y_conv, w_ga, w_gc, b_gates, w_att_out, w_conv_out, w_o, g_ffn)


FFN_SLICE = 256


def _ffn_kernel(indices, h_ref, wg_ref, wv_ref, cwg_ref, cwv_ref, wd_ref, o_ref,
                carry_ref, pre_ref, act_ref, *, tm):
    i, f = indices

    @pl.when(i == 0)
    def _():
        carry_ref[f] = jnp.zeros(carry_ref.shape[1:], F32)

    @pl.when(f == 0)
    def _():
        o_ref[...] = jnp.zeros(o_ref.shape, F32)

    h = h_ref[...]
    n_slices = wg_ref.shape[1] // FFN_SLICE
    slices = [slice(a * FFN_SLICE, (a + 1) * FFN_SLICE) for a in range(n_slices)]
    for a, sl in enumerate(slices):
        for b, (w_ref, cw_ref) in enumerate(((wg_ref, cwg_ref), (wv_ref, cwv_ref))):
            pre_ref[b, a, 0:CARRY_ROWS] = carry_ref[f, b, :, sl]
            pre_ref[b, a, CARRY_ROWS:CARRY_ROWS + tm] = jnp.dot(h, w_ref[:, sl],
                                                                preferred_element_type=F32)
            carry_ref[f, b, :, sl] = pre_ref[b, a, tm:tm + CARRY_ROWS]
    for a, sl in enumerate(slices):
        u = []
        for b, cw_ref in enumerate((cwg_ref, cwv_ref)):
            w = cw_ref[:, sl]
            u.append(w[0:1] * pre_ref[b, a, CARRY_ROWS - 2:CARRY_ROWS - 2 + tm]
                     + w[1:2] * pre_ref[b, a, CARRY_ROWS - 1:CARRY_ROWS - 1 + tm]
                     + w[2:3] * pre_ref[b, a, CARRY_ROWS:CARRY_ROWS + tm])
        act_ref[a] = (u[0] * jax.nn.sigmoid(u[0]) * u[1]).astype(BF16)
    for a, sl in enumerate(slices):
        for n in range(o_ref.shape[1] // FFN_SLICE):
            cs = slice(n * FFN_SLICE, (n + 1) * FFN_SLICE)
            o_ref[:, cs] += jnp.dot(act_ref[a], wd_ref[sl, cs], preferred_element_type=F32)


def _ffn_call(h2, w_up, w_ffn_conv, w_down, *, tm, tf):
    s = h2.shape[0]
    n_f = D_FF // tf
    in_specs = [
        pl.BlockSpec((tm, D_MODEL), lambda i, f: (i, 0)),
        pl.BlockSpec((D_MODEL, tf), lambda i, f: (0, f)),
        pl.BlockSpec((D_MODEL, tf), lambda i, f: (0, n_f + f)),
        pl.BlockSpec((3, tf), lambda i, f: (0, f)),
        pl.BlockSpec((3, tf), lambda i, f: (0, n_f + f)),
        pl.BlockSpec((tf, D_MODEL), lambda i, f: (f, 0)),
    ]
    out_spec = pl.BlockSpec((tm, D_MODEL), lambda i, f: (i, 0))

    def whole_call(*refs):
        operands, count_ref, scratch = refs[:len(in_specs) + 1], refs[len(in_specs) + 1], refs[len(in_specs) + 2:]
        count_ref[0] = 0

        def step(*blocks):
            n = count_ref[0]
            count_ref[0] = n + 1
            _ffn_kernel((n // n_f, n % n_f), *blocks, *scratch, tm=tm)

        pltpu.emit_pipeline(step, grid=(s // tm, n_f), in_specs=in_specs, out_specs=[out_spec])(*operands)

    anywhere = pl.BlockSpec(memory_space=pl.ANY)
    return pl.pallas_call(
        whole_call,
        in_specs=[anywhere] * len(in_specs),
        out_specs=anywhere,
        out_shape=jax.ShapeDtypeStruct((s, D_MODEL), F32),
        scratch_shapes=[pltpu.SMEM((1,), jnp.int32),
                        pltpu.VMEM((n_f, 2, CARRY_ROWS, tf), F32),
                        pltpu.VMEM((2, tf // FFN_SLICE, CARRY_ROWS + tm, FFN_SLICE), F32),
                        pltpu.VMEM((tf // FFN_SLICE, tm, FFN_SLICE), BF16)],
        compiler_params=pltpu.CompilerParams(vmem_limit_bytes=VMEM_LIMIT),
        name="conv_ffn",
    )(h2, w_up, w_up, w_ffn_conv, w_ffn_conv, w_down)


def _ple_kernel(x_ref, y_ref, p_ref, wple_ref, gple_ref, gpg_ref, wpg_ref, bpg_ref, o_ref):
    xf = x_ref[...] + y_ref[...]
    pe = _rms_rows(jnp.dot(p_ref[...].astype(BF16), wple_ref[...], preferred_element_type=F32),
                   gple_ref[...])
    hn = _rms_rows(xf, gpg_ref[...]).astype(BF16)
    pg = jax.nn.sigmoid(jnp.dot(hn, wpg_ref[...], preferred_element_type=F32) + bpg_ref[...])
    o_ref[...] = xf + pg * pe


def _ple_call(x1, y_ffn, p, w_ple, g_ple, g_pg, w_pg, b_pg, *, tm):
    s = x1.shape[0]
    const = lambda i: (0, 0)
    return pl.pallas_call(
        _ple_kernel,
        grid=(s // tm,),
        in_specs=[
            pl.BlockSpec((tm, D_MODEL), lambda i: (i, 0)),
            pl.BlockSpec((tm, D_MODEL), lambda i: (i, 0)),
            pl.BlockSpec((tm, PLE_DIM), lambda i: (i, 0)),
            _resident((PLE_DIM, D_MODEL), const),
            _resident((1, D_MODEL), const),
            _resident((1, D_MODEL), const),
            _resident((D_MODEL, D_MODEL), const),
            _resident((1, D_MODEL), const),
        ],
        out_specs=pl.BlockSpec((tm, D_MODEL), lambda i: (i, 0)),
        out_shape=jax.ShapeDtypeStruct((s, D_MODEL), F32),
        compiler_params=_params("arbitrary"),
        name="ple_gate",
    )(x1, y_ffn, p, w_ple, g_ple, g_pg, w_pg, b_pg)


def _layer(depth_index, x, p, g_mix, w_in, b_gates, g_q, g_k, lam_rows, g_sub, w_conv_mix, w_att_out,
           w_conv_out, w_o, g_ffn, w_up, w_ffn_conv, w_down, w_ple, g_ple, g_pg, w_pg, b_pg):
    lambda_init = 0.8 - 0.6 * math.exp(-0.3 * depth_index)
    row = lambda a: a.reshape(1, -1).astype(F32)

    g_mix = row(g_mix)
    qkv, y_conv = _inproj_call(x, g_mix, w_in.astype(F32), _qk_table(g_q, g_k),
                               w_conv_mix.astype(F32), tm=1024)
    later = [w_in, w_in, w_att_out, w_conv_out, w_o, w_up, w_down, w_ple, w_pg]
    jobs = ([(D_MODEL, OFF_GA // D_MODEL), (D_MODEL, OFF_GC // D_MODEL)]
            + [(w.shape[1], 0) for w in later[2:]])
    y_att, w_ga, w_gc, w_att_out, w_conv_out, w_o, w_up, w_down, w_ple, w_pg = _attention(
        lam_rows, row(g_sub), g_q, g_k, qkv, [w.astype(F32) for w in later], jobs,
        t=1024, lambda_init=lambda_init)
    x1, h2 = _merge_call(x, g_mix, y_att, y_conv, w_ga, w_gc, b_gates.astype(F32),
                         w_att_out, w_conv_out, w_o, row(g_ffn), tm=256)
    y_ffn = _ffn_call(h2, w_up, w_ffn_conv.astype(F32), w_down, tm=1024, tf=512)
    return _ple_call(x1, y_ffn, p, w_ple, row(g_ple), row(g_pg), w_pg, row(b_pg), tm=512)


def kernel(x, p, g_mix, w_in, b_gates, g_q, g_k, lam_q1, lam_k1, lam_q2, lam_k2, g_sub, w_conv_mix,
           w_att_out, w_conv_out, w_o, g_ffn, w_up, w_ffn_conv, w_down, w_ple, g_ple, g_pg, w_pg,
           b_pg):
    batch, seq, d_model = x.shape
    depth = p.shape[0]
    outs = []
    for b in range(batch):
        xb = x.reshape(seq, d_model) if batch == 1 else x[b]
        for i in range(depth):
            lam_rows = jnp.stack([lam_q1[i], lam_k1[i], lam_q2[i], lam_k2[i]]).astype(F32)
            pb = p.reshape(seq, PLE_DIM) if batch == 1 and depth == 1 else p[i, b]
            xb = _layer(i, xb, pb, g_mix[i], w_in[i], b_gates[i], g_q[i], g_k[i], lam_rows,
                        g_sub[i], w_conv_mix[i], w_att_out[i], w_conv_out[i], w_o[i], g_ffn[i],
                        w_up[i], w_ffn_conv[i], w_down[i], w_ple[i], g_ple[i], g_pg[i], w_pg[i],
                        b_pg[i])
        outs.append(xb)
    return outs[0].reshape(x.shape) if batch == 1 else jnp.stack(outs)
```

```python
import functools
import math

import jax
import jax.numpy as jnp
import numpy as np
from jax import lax
from jax.experimental import pallas as pl
from jax.experimental.pallas import tpu as pltpu

F32 = jnp.float32
BF16 = jnp.bfloat16

D_MODEL = 2048
ATT_HEADS = 8
ATT_HEAD_DIM = 64
ATT_V_DIM = 2 * ATT_HEAD_DIM
ATT_QK_WIDTH = ATT_HEADS * 2 * ATT_HEAD_DIM
ATT_WIDTH = ATT_HEADS * ATT_V_DIM
CONV_WIDTH = 1024
D_FF = 5632
PLE_DIM = 256
EPS = 1e-6

LANES = 128
CARRY_ROWS = 8
MASK_VALUE = -1e30
VMEM_LIMIT = 56 * 1024 * 1024

OFF_GA = 2 * ATT_QK_WIDTH + ATT_WIDTH + 3 * CONV_WIDTH
OFF_GC = OFF_GA + D_MODEL

NT_DIMS = (((1,), (1,)), ((), ()))


def _rms_rows(xf, g):
    ms = jnp.mean(xf * xf, axis=-1, keepdims=True)
    return xf * lax.rsqrt(ms + EPS) * g


def _params(*sem):
    return pltpu.CompilerParams(dimension_semantics=sem, vmem_limit_bytes=VMEM_LIMIT)


def _pipelined_call(body, *, grid, in_specs, out_specs, out_shape, compiler_params, name):
    out_specs = list(out_specs)

    def whole_call(*refs):
        pltpu.emit_pipeline(body, grid=grid, in_specs=in_specs, out_specs=out_specs)(*refs)

    anywhere = pl.BlockSpec(memory_space=pl.ANY)
    return pl.pallas_call(
        whole_call,
        in_specs=[anywhere] * len(in_specs),
        out_specs=[anywhere] * len(out_specs),
        out_shape=list(out_shape),
        compiler_params=pltpu.CompilerParams(vmem_limit_bytes=compiler_params.vmem_limit_bytes),
        name=name,
    )


def _resident(shape, index_map):
    return pl.BlockSpec(shape, index_map, pipeline_mode=pl.Buffered(1))


def _shift_rows(u, prev, rows):
    u1 = jnp.where(rows == 0, prev[CARRY_ROWS - 1:CARRY_ROWS], pltpu.roll(u, 1, 0))
    u2 = jnp.where(rows == 0, prev[CARRY_ROWS - 2:CARRY_ROWS - 1],
                   jnp.where(rows == 1, prev[CARRY_ROWS - 1:CARRY_ROWS], pltpu.roll(u, 2, 0)))
    return u1, u2


def _causal_conv3(u, w, prev):
    rows = lax.broadcasted_iota(jnp.int32, u.shape, 0)
    u1, u2 = _shift_rows(u, prev, rows)
    return w[0:1] * u2 + w[1:2] * u1 + w[2:3] * u


ROW_GAIN, ROW_ONES, ROW_SLOPE, ROW_HI, ROW_MID, ROW_LO, TABLE_ROWS = 0, 1, 2, 3, 4, 5, 8
V_GROUP = 2 * LANES
PROJ_TILE = 512
N_QK_TILES = 2 * ATT_QK_WIDTH // PROJ_TILE
N_QKV_TILES = N_QK_TILES + ATT_WIDTH // PROJ_TILE
N_CONV_TILES = CONV_WIDTH // PROJ_TILE
N_PROJ_TILES = N_QKV_TILES + 3 * N_CONV_TILES
TILE_V0 = N_QK_TILES
TILE_CB0 = N_QKV_TILES
TILE_CX0 = N_QKV_TILES + 2 * N_CONV_TILES
QKV_OUT_TILE = 2 * PROJ_TILE
QKV_OUT_WIDTH = N_QKV_TILES * QKV_OUT_TILE


def _qk_tile(acc, tab_ref, o_ref, first_pos):
    tm = acc.shape[0]
    lane = lax.broadcasted_iota(jnp.int32, (tm, LANES), 1)
    pos = (first_pos + lax.broadcasted_iota(jnp.int32, (tm, LANES), 0)).astype(F32)
    first = lane < ATT_HEAD_DIM
    for c in range(acc.shape[1] // LANES):
        t = acc[:, c * LANES:(c + 1) * LANES]
        sl = slice(c * LANES, (c + 1) * LANES)
        sq = t * t
        ss0 = jnp.sum(jnp.where(first, sq, 0.0), axis=-1, keepdims=True)
        ss1 = jnp.sum(jnp.where(first, 0.0, sq), axis=-1, keepdims=True)
        r = jnp.where(first, lax.rsqrt(ss0 / ATT_HEAD_DIM + EPS), lax.rsqrt(ss1 / ATT_HEAD_DIM + EPS))
        tnorm = t * r * tab_ref[ROW_GAIN:ROW_GAIN + 1, sl]
        bias = tab_ref[ROW_SLOPE:ROW_SLOPE + 1, sl] * pos
        hi = bias.astype(BF16).astype(F32)
        rem = bias - hi
        mid = rem.astype(BF16).astype(F32)
        lo = rem - mid
        aug = (tab_ref[ROW_ONES:ROW_ONES + 1, sl] + tab_ref[ROW_HI:ROW_HI + 1, sl] * hi
               + tab_ref[ROW_MID:ROW_MID + 1, sl] * mid + tab_ref[ROW_LO:ROW_LO + 1, sl] * lo)
        comp0 = jnp.where(first, tnorm, aug)
        comp1 = jnp.where(first, pltpu.roll(tnorm, ATT_HEAD_DIM, 1), aug)
        out0 = 2 * c * LANES
        o_ref[:, out0:out0 + LANES] = comp0.astype(BF16)
        o_ref[:, out0 + LANES:out0 + 2 * LANES] = comp1.astype(BF16)


def _v_tile(acc, o_ref):
    tm = acc.shape[0]
    ones = jnp.ones((tm, V_GROUP - ATT_V_DIM), BF16)
    for a in range(PROJ_TILE // ATT_V_DIM):
        o_ref[:, a * V_GROUP:a * V_GROUP + ATT_V_DIM] = acc[:, a * ATT_V_DIM:(a + 1) * ATT_V_DIM].astype(BF16)
        o_ref[:, a * V_GROUP + ATT_V_DIM:(a + 1) * V_GROUP] = ones


def _inproj_kernel(x_ref, g_ref, w_ref, tab_ref, cw_ref, qkv_ref, conv_ref,
                   h_ref, carry_ref, park_ref, hold_ref, *, tm):
    i = pl.program_id(0)
    j = pl.program_id(1)

    def project(dst_ref, slot):
        dst_ref[slot] = jnp.dot(h_ref[...], w_ref[...].astype(BF16), preferred_element_type=F32)

    def finish_conv(jc, cx_slot):
        z = hold_ref[N_CONV_TILES + jc] * park_ref[cx_slot]
        y = _causal_conv3(z, cw_ref[...], carry_ref[jc])
        conv_ref[...] = (hold_ref[jc] * y).astype(BF16)
        carry_ref[jc] = z[tm - CARRY_ROWS:tm]

    @pl.when(j == 0)
    def _():
        @pl.when(i == 0)
        def _():
            carry_ref[...] = jnp.zeros(carry_ref.shape, F32)

        h_ref[...] = _rms_rows(x_ref[...], g_ref[...]).astype(BF16)
        project(park_ref, 0)

    @pl.when(jnp.logical_and(j >= 1, j <= TILE_V0))
    def _():
        _qk_tile(park_ref[(j - 1) % 2], tab_ref, qkv_ref, i * tm)
        project(park_ref, j % 2)

    @pl.when(j == TILE_V0 + 1)
    def _():
        _v_tile(park_ref[TILE_V0 % 2], qkv_ref)
        project(park_ref, (TILE_V0 + 1) % 2)

    @pl.when(j == TILE_CB0)
    def _():
        _v_tile(park_ref[(TILE_V0 + 1) % 2], qkv_ref)
        project(hold_ref, 0)

    @pl.when(jnp.logical_and(j > TILE_CB0, j < TILE_CX0))
    def _():
        project(hold_ref, j - TILE_CB0)

    @pl.when(j == TILE_CX0)
    def _():
        project(park_ref, 0)

    @pl.when(j == TILE_CX0 + 1)
    def _():
        finish_conv(0, 0)
        project(park_ref, 1)

    @pl.when(j == TILE_CX0 + 2)
    def _():
        finish_conv(1, 1)


def _qk_table(g_q, g_k):
    n_groups = ATT_QK_WIDTH // ATT_HEAD_DIM
    slopes = jnp.asarray(2.0 ** (-8.0 * np.arange(1, ATT_HEADS + 1) / ATT_HEADS), F32)
    slope_cols = jnp.repeat(slopes, 2 * ATT_HEAD_DIM)
    lane = jnp.tile(jnp.arange(LANES), ATT_HEADS) - ATT_HEAD_DIM
    at = lambda n: (lane == n).astype(F32)
    zeros = jnp.zeros(ATT_QK_WIDTH, F32)
    both = lambda q_part, k_part: jnp.concatenate([q_part, k_part])
    rows = [None] * TABLE_ROWS
    rows[ROW_GAIN] = both(jnp.tile(g_q.astype(F32) * ATT_HEAD_DIM ** -0.5, n_groups),
                          jnp.tile(g_k.astype(F32), n_groups))
    rows[ROW_ONES] = both(at(0) + at(1) + at(2), at(3) + at(4) + at(5))
    rows[ROW_SLOPE] = both(-slope_cols, slope_cols)
    rows[ROW_HI] = both(at(3), at(0))
    rows[ROW_MID] = both(at(4), at(1))
    rows[ROW_LO] = both(at(5), at(2))
    return jnp.stack([both(zeros, zeros) if r is None else r for r in rows])


def _inproj_call(x, g_mix, w_in, table, w_conv, *, tm):
    s = x.shape[0]
    finished = lambda j, first, count: jnp.clip(j - 1 - first, 0, count - 1)
    return pl.pallas_call(
        functools.partial(_inproj_kernel, tm=tm),
        grid=(s // tm, N_PROJ_TILES + 1),
        in_specs=[
            pl.BlockSpec((tm, D_MODEL), lambda i, j: (i, 0)),
            pl.BlockSpec((1, D_MODEL), lambda i, j: (0, 0)),
            pl.BlockSpec((D_MODEL, PROJ_TILE), lambda i, j: (0, jnp.minimum(j, N_PROJ_TILES - 1))),
            pl.BlockSpec((TABLE_ROWS, PROJ_TILE), lambda i, j: (0, finished(j, 0, N_QK_TILES))),
            pl.BlockSpec((3, PROJ_TILE), lambda i, j: (0, finished(j, TILE_CX0, N_CONV_TILES))),
        ],
        out_specs=[pl.BlockSpec((tm, QKV_OUT_TILE), lambda i, j: (i, finished(j, 0, N_QKV_TILES))),
                   pl.BlockSpec((tm, PROJ_TILE),
                                lambda i, j: (i, finished(j, TILE_CX0, N_CONV_TILES)))],
        out_shape=[jax.ShapeDtypeStruct((s, QKV_OUT_WIDTH), BF16),
                   jax.ShapeDtypeStruct((s, CONV_WIDTH), BF16)],
        scratch_shapes=[pltpu.VMEM((tm, D_MODEL), BF16),
                        pltpu.VMEM((N_CONV_TILES, CARRY_ROWS, PROJ_TILE), F32),
                        pltpu.VMEM((2, tm, PROJ_TILE), F32),
                        pltpu.VMEM((2 * N_CONV_TILES, tm, PROJ_TILE), F32)],
        compiler_params=_params("arbitrary", "arbitrary"),
        name="in_proj",
    )(x, g_mix, w_in, table, w_conv)


def _attn_kernel(lam_ref, gsub_ref, q0_ref, q1_ref, k0_ref, k1_ref, v_ref, o_ref,
                 m_ref, l_ref, acc_ref, *, t, lambda_init, side_work):
    i = pl.program_id(1)
    side_work()
    m_ref[...] = jnp.full(m_ref.shape, MASK_VALUE, F32)
    l_ref[...] = jnp.zeros(l_ref.shape, F32)
    acc_ref[...] = jnp.zeros(acc_ref.shape, F32)

    q = (q0_ref[...], q1_ref[...])
    k_refs = (k0_ref, k1_ref)
    n_chunks = t // LANES

    def tile(j, masked):
        start = pl.multiple_of(j * t, t)
        v = v_ref[pl.ds(start, t), :]
        if masked:
            keep = (lax.broadcasted_iota(jnp.int32, (t, t), 1)
                    <= lax.broadcasted_iota(jnp.int32, (t, t), 0))
        for c in range(2):
            k = k_refs[c][pl.ds(start, t), :]
            s = lax.dot_general(q[c], k, NT_DIMS, preferred_element_type=F32)
            if masked:
                s = jnp.where(keep, s, MASK_VALUE)
            chunks = [s[:, a * LANES:(a + 1) * LANES] for a in range(n_chunks)]
            m_prev = m_ref[c]
            m_new = jnp.maximum(m_prev, jnp.max(functools.reduce(jnp.maximum, chunks),
                                                axis=1, keepdims=True))
            alpha = jnp.exp(m_prev - m_new)
            ps = [jnp.exp(ch - m_new) for ch in chunks]
            l_ref[c] = alpha * l_ref[c] + functools.reduce(jnp.add, ps)
            p = jnp.concatenate([x.astype(BF16) for x in ps], axis=1)
            acc_ref[c] = alpha * acc_ref[c] + jnp.dot(p, v, preferred_element_type=F32)
            m_ref[c] = m_new

    def body(j, carry):
        tile(j, False)
        return carry

    lax.fori_loop(0, i, body, 0)
    tile(i, True)

    l0 = jnp.sum(l_ref[0], axis=1, keepdims=True)
    l1 = jnp.sum(l_ref[1], axis=1, keepdims=True)
    _attn_finish(acc_ref[0], l0, acc_ref[1], l1, lam_ref, gsub_ref, o_ref, lambda_init)


def _attn_finish(acc0, l0, acc1, l1, lam_ref, gsub_ref, o_ref, lambda_init):
    lam_rows = lam_ref[...]
    lam = (jnp.exp(jnp.sum(lam_rows[0:1] * lam_rows[1:2], axis=-1, keepdims=True))
           - jnp.exp(jnp.sum(lam_rows[2:3] * lam_rows[3:4], axis=-1, keepdims=True))
           + lambda_init)
    o = acc0 / l0 - lam * (acc1 / l1)
    o_ref[...] = (_rms_rows(o, gsub_ref[...]) * (1.0 - lambda_init)).astype(BF16)


def _attn_bounded_kernel(lam_ref, gsub_ref, q0_ref, q1_ref, k0_ref, k1_ref, v_ref, o_ref, acc_ref,
                         *, t, lambda_init, wide, side_work):
    i = pl.program_id(1)
    q = (q0_ref[...], q1_ref[...])
    k_refs = (k0_ref, k1_ref)

    def tile(first, n_tiles, diagonal_last):
        start = pl.multiple_of(first * t, t)
        n_keys = n_tiles * t
        if not diagonal_last:
            v = v_ref[pl.ds(start, n_keys), :]
            for c in range(2):
                k = k_refs[c][pl.ds(start, n_keys), :]
                s = lax.dot_general(q[c], k, NT_DIMS, preferred_element_type=F32)
                acc_ref[c] += jnp.dot(jnp.exp(s).astype(BF16), v, preferred_element_type=F32)
            return
        half = t // 2
        keep = (lax.broadcasted_iota(jnp.int32, (half, half), 1)
                <= lax.broadcasted_iota(jnp.int32, (half, half), 0))
        for c in range(2):
            for rows, n_seen in ((slice(0, half), n_keys - half), (slice(half, t), n_keys)):
                k = k_refs[c][pl.ds(start, n_seen), :]
                s = lax.dot_general(q[c][rows], k, NT_DIMS, preferred_element_type=F32)
                parts = [jnp.where(keep, s[:, n_seen - half:], MASK_VALUE)]
                if n_seen > half:
                    parts.insert(0, s[:, :n_seen - half])
                p = jnp.exp(jnp.concatenate(parts, axis=1)).astype(BF16)
                acc_ref[c, rows] = jnp.dot(p, v_ref[pl.ds(start, n_seen), :],
                                           preferred_element_type=F32)

    n_wide = i // wide
    for r in range(wide):
        @pl.when(i - n_wide * wide == r)
        def _(r=r):
            side_work()
            tile(i - r, r + 1, True)

    def wide_body(j, carry):
        tile(j * wide, wide, False)
        return carry

    lax.fori_loop(0, n_wide, wide_body, 0)

    acc0 = acc_ref[0]
    acc1 = acc_ref[1]
    _attn_finish(acc0[:, :ATT_V_DIM], acc0[:, ATT_V_DIM:],
                 acc1[:, :ATT_V_DIM], acc1[:, ATT_V_DIM:],
                 lam_ref, gsub_ref, o_ref, lambda_init)


ATT_WIDE_TILES = 4


BF16_ROWS = 16
N_ATTN_INPUTS = 7


def _with_weight_casts(attn_body, n):
    def kernel(*refs):
        ins, refs = refs[:N_ATTN_INPUTS], refs[N_ATTN_INPUTS:]
        cast_in, o_ref, cast_out, scratch = refs[:n], refs[n], refs[n + 1:2 * n + 1], refs[2 * n + 1:]

        def cast_weights():
            for src, dst in zip(cast_in, cast_out):
                dst[...] = src[...].astype(BF16)

        attn_body(*ins, o_ref, *scratch, side_work=cast_weights)

    return kernel


def _attn_call(lam_rows, g_sub, qkv, *cast_arrays, cast_jobs, t, lambda_init, bounded):
    s = qkv.shape[0]
    n_q = s // t
    n_steps = ATT_HEADS * n_q
    k_off = 2 * ATT_HEADS
    v_off = 2 * ATT_QK_WIDTH // ATT_HEAD_DIM
    if bounded:
        body = functools.partial(_attn_bounded_kernel, wide=ATT_WIDE_TILES)
        v_spec = pl.BlockSpec((s, V_GROUP), lambda h, i: (0, v_off * LANES // V_GROUP + h))
        scratch = [pltpu.VMEM((2, t, V_GROUP), F32)]
    else:
        body = _attn_kernel
        v_spec = pl.BlockSpec((s, ATT_V_DIM), lambda h, i: (0, v_off + (V_GROUP // ATT_V_DIM) * h))
        scratch = [pltpu.VMEM((2, t, LANES), F32),
                   pltpu.VMEM((2, t, LANES), F32),
                   pltpu.VMEM((2, t, ATT_V_DIM), F32)]
    cast_in_specs, cast_out_specs, cast_out_shapes = [], [], []
    for arr, (width, col_block) in zip(cast_arrays, cast_jobs):
        rows = arr.shape[0]
        col_blocks = 1
        while rows % (BF16_ROWS * n_steps // col_blocks):
            col_blocks *= 2
        block = (rows * col_blocks // n_steps, width // col_blocks)
        assert block[1] % LANES == 0 and block[1] * col_blocks == width
        cast_in_specs.append(pl.BlockSpec(
            block, lambda h, i, cb=col_blocks, c0=col_block * col_blocks:
            ((h * n_q + i) // cb, c0 + (h * n_q + i) % cb)))
        cast_out_specs.append(pl.BlockSpec(
            block, lambda h, i, cb=col_blocks: ((h * n_q + i) // cb, (h * n_q + i) % cb)))
        cast_out_shapes.append(jax.ShapeDtypeStruct((rows, width), BF16))
    return pl.pallas_call(
        _with_weight_casts(functools.partial(body, t=t, lambda_init=lambda_init), len(cast_jobs)),
        grid=(ATT_HEADS, n_q),
        in_specs=[
            pl.BlockSpec((4, ATT_HEAD_DIM), lambda h, i: (0, 0)),
            pl.BlockSpec((1, ATT_V_DIM), lambda h, i: (0, 0)),
            pl.BlockSpec((t, LANES), lambda h, i: (i, 2 * h)),
            pl.BlockSpec((t, LANES), lambda h, i: (i, 2 * h + 1)),
            pl.BlockSpec((s, LANES), lambda h, i: (0, k_off + 2 * h)),
            pl.BlockSpec((s, LANES), lambda h, i: (0, k_off + 2 * h + 1)),
            v_spec,
        ] + cast_in_specs,
        out_specs=[pl.BlockSpec((t, ATT_V_DIM), lambda h, i: (i, h))] + cast_out_specs,
        out_shape=[jax.ShapeDtypeStruct((s, ATT_WIDTH), BF16)] + cast_out_shapes,
        scratch_shapes=scratch,
        compiler_params=_params("arbitrary", "arbitrary"),
        name="diff_attn_bounded" if bounded else "diff_attn",
    )(lam_rows, g_sub, qkv, qkv, qkv, qkv, qkv, *cast_arrays)


SCORE_BOUND_LIMIT = 40.0


def _attention(lam_rows, g_sub, g_q, g_k, qkv, cast_arrays, cast_jobs, *, t, lambda_init):
    bound = (1.01 * ATT_HEAD_DIM ** 0.5
             * jnp.max(jnp.abs(g_q.astype(F32))) * jnp.max(jnp.abs(g_k.astype(F32))))
    call = functools.partial(_attn_call, cast_jobs=cast_jobs, t=t, lambda_init=lambda_init)
    return lax.cond(bound <= SCORE_BOUND_LIMIT,
                    functools.partial(call, bounded=True),
                    functools.partial(call, bounded=False),
                    lam_rows, g_sub, qkv, *cast_arrays)


def _merge_kernel(x_ref, g_ref, ya_ref, yc_ref, wga_ref, wgc_ref, bg_ref, wa_ref, wc_ref, wo_ref,
                  gffn_ref, x1_ref, h2_ref):
    xf = x_ref[...]
    h = _rms_rows(xf, g_ref[...]).astype(BF16)
    bg = bg_ref[...]
    gate_a = jax.nn.sigmoid(jnp.dot(h, wga_ref[...], preferred_element_type=F32) + bg[0:1])
    merged = gate_a * jnp.dot(ya_ref[...], wa_ref[...], preferred_element_type=F32)
    gate_c = jax.nn.sigmoid(jnp.dot(h, wgc_ref[...], preferred_element_type=F32) + bg[1:2])
    merged = merged + gate_c * jnp.dot(yc_ref[...], wc_ref[...], preferred_element_type=F32)
    x1 = xf + jnp.dot(merged.astype(BF16), wo_ref[...], preferred_element_type=F32)
    x1_ref[...] = x1
    h2_ref[...] = _rms_rows(x1, gffn_ref[...]).astype(BF16)


def _merge_call(x, g_mix, y_att, y_conv, w_ga, w_gc, b_gates, w_att_out, w_conv_out, w_o, g_ffn,
                *, tm):
    s = x.shape[0]
    const = lambda i: (0, 0)
    return _pipelined_call(
        _merge_kernel,
        grid=(s // tm,),
        in_specs=[
            pl.BlockSpec((tm, D_MODEL), lambda i: (i, 0)),
            _resident((1, D_MODEL), const),
            pl.BlockSpec((tm, ATT_WIDTH), lambda i: (i, 0)),
            pl.BlockSpec((tm, CONV_WIDTH), lambda i: (i, 0)),
            _resident((D_MODEL, D_MODEL), const),
            _resident((D_MODEL, D_MODEL), const),
            _resident((2, D_MODEL), const),
            _resident((ATT_WIDTH, D_MODEL), const),
            _resident((CONV_WIDTH, D_MODEL), const),
            _resident((D_MODEL, D_MODEL), const),
            _resident((1, D_MODEL), const),
        ],
        out_specs=[pl.BlockSpec((tm, D_MODEL), lambda i: (i, 0)),
                   pl.BlockSpec((tm, D_MODEL), lambda i: (i, 0))],
        out_shape=[jax.ShapeDtypeStruct((s, D_MODEL), F32),
                   jax.ShapeDtypeStruct((s, D_MODEL), BF16)],
        compiler_params=_params("arbitrary"),
        name="merge",
    )(x, g_mix, y_att, y_conv, w_ga, w_gc, b_gates, w_att_out, w_conv_out, w_o, g_ffn)


FFN_SLICE = 256


def _ffn_kernel(indices, h_ref, wg_ref, wv_ref, cwg_ref, cwv_ref, wd_ref, o_ref,
                carry_ref, pre_ref, act_ref, *, tm):
    i, f = indices

    @pl.when(i == 0)
    def _():
        carry_ref[f] = jnp.zeros(carry_ref.shape[1:], F32)

    @pl.when(f == 0)
    def _():
        o_ref[...] = jnp.zeros(o_ref.shape, F32)

    h = h_ref[...]
    n_slices = wg_ref.shape[1] // FFN_SLICE
    slices = [slice(a * FFN_SLICE, (a + 1) * FFN_SLICE) for a in range(n_slices)]
    for a, sl in enumerate(slices):
        for b, (w_ref, cw_ref) in enumerate(((wg_ref, cwg_ref), (wv_ref, cwv_ref))):
            pre_ref[b, a, 0:CARRY_ROWS] = carry_ref[f, b, :, sl]
            pre_ref[b, a, CARRY_ROWS:CARRY_ROWS + tm] = jnp.dot(h, w_ref[:, sl],
                                                                preferred_element_type=F32)
            carry_ref[f, b, :, sl] = pre_ref[b, a, tm:tm + CARRY_ROWS]
    for a, sl in enumerate(slices):
        u = []
        for b, cw_ref in enumerate((cwg_ref, cwv_ref)):
            w = cw_ref[:, sl]
            u.append(w[0:1] * pre_ref[b, a, CARRY_ROWS - 2:CARRY_ROWS - 2 + tm]
                     + w[1:2] * pre_ref[b, a, CARRY_ROWS - 1:CARRY_ROWS - 1 + tm]
                     + w[2:3] * pre_ref[b, a, CARRY_ROWS:CARRY_ROWS + tm])
        act_ref[a] = (u[0] * jax.nn.sigmoid(u[0]) * u[1]).astype(BF16)
    for a, sl in enumerate(slices):
        for n in range(o_ref.shape[1] // FFN_SLICE):
            cs = slice(n * FFN_SLICE, (n + 1) * FFN_SLICE)
            o_ref[:, cs] += jnp.dot(act_ref[a], wd_ref[sl, cs], preferred_element_type=F32)


def _ffn_call(h2, w_up, w_ffn_conv, w_down, *, tm, tf):
    s = h2.shape[0]
    n_f = D_FF // tf
    in_specs = [
        pl.BlockSpec((tm, D_MODEL), lambda i, f: (i, 0)),
        pl.BlockSpec((D_MODEL, tf), lambda i, f: (0, f)),
        pl.BlockSpec((D_MODEL, tf), lambda i, f: (0, n_f + f)),
        pl.BlockSpec((3, tf), lambda i, f: (0, f)),
        pl.BlockSpec((3, tf), lambda i, f: (0, n_f + f)),
        pl.BlockSpec((tf, D_MODEL), lambda i, f: (f, 0)),
    ]
    out_spec = pl.BlockSpec((tm, D_MODEL), lambda i, f: (i, 0))

    def whole_call(*refs):
        operands, count_ref, scratch = refs[:len(in_specs) + 1], refs[len(in_specs) + 1], refs[len(in_specs) + 2:]
        count_ref[0] = 0

        def step(*blocks):
            n = count_ref[0]
            count_ref[0] = n + 1
            _ffn_kernel((n // n_f, n % n_f), *blocks, *scratch, tm=tm)

        pltpu.emit_pipeline(step, grid=(s // tm, n_f), in_specs=in_specs, out_specs=[out_spec])(*operands)

    anywhere = pl.BlockSpec(memory_space=pl.ANY)
    return pl.pallas_call(
        whole_call,
        in_specs=[anywhere] * len(in_specs),
        out_specs=anywhere,
        out_shape=jax.ShapeDtypeStruct((s, D_MODEL), F32),
        scratch_shapes=[pltpu.SMEM((1,), jnp.int32),
                        pltpu.VMEM((n_f, 2, CARRY_ROWS, tf), F32),
                        pltpu.VMEM((2, tf // FFN_SLICE, CARRY_ROWS + tm, FFN_SLICE), F32),
                        pltpu.VMEM((tf // FFN_SLICE, tm, FFN_SLICE), BF16)],
        compiler_params=pltpu.CompilerParams(vmem_limit_bytes=VMEM_LIMIT),
        name="conv_ffn",
    )(h2, w_up, w_up, w_ffn_conv, w_ffn_conv, w_down)


def _ple_kernel(x_ref, y_ref, p_ref, wple_ref, gple_ref, gpg_ref, wpg_ref, bpg_ref, o_ref):
    xf = x_ref[...] + y_ref[...]
    pe = _rms_rows(jnp.dot(p_ref[...].astype(BF16), wple_ref[...], preferred_element_type=F32),
                   gple_ref[...])
    hn = _rms_rows(xf, gpg_ref[...]).astype(BF16)
    pg = jax.nn.sigmoid(jnp.dot(hn, wpg_ref[...], preferred_element_type=F32) + bpg_ref[...])
    o_ref[...] = xf + pg * pe


def _ple_call(x1, y_ffn, p, w_ple, g_ple, g_pg, w_pg, b_pg, *, tm):
    s = x1.shape[0]
    const = lambda i: (0, 0)
    return pl.pallas_call(
        _ple_kernel,
        grid=(s // tm,),
        in_specs=[
            pl.BlockSpec((tm, D_MODEL), lambda i: (i, 0)),
            pl.BlockSpec((tm, D_MODEL), lambda i: (i, 0)),
            pl.BlockSpec((tm, PLE_DIM), lambda i: (i, 0)),
            _resident((PLE_DIM, D_MODEL), const),
            _resident((1, D_MODEL), const),
            _resident((1, D_MODEL), const),
            _resident((D_MODEL, D_MODEL), const),
            _resident((1, D_MODEL), const),
        ],
        out_specs=pl.BlockSpec((tm, D_MODEL), lambda i: (i, 0)),
        out_shape=jax.ShapeDtypeStruct((s, D_MODEL), F32),
        compiler_params=_params("arbitrary"),
        name="ple_gate",
    )(x1, y_ffn, p, w_ple, g_ple, g_pg, w_pg, b_pg)


def _layer(depth_index, x, p, g_mix, w_in, b_gates, g_q, g_k, lam_rows, g_sub, w_conv_mix, w_att_out,
           w_conv_out, w_o, g_ffn, w_up, w_ffn_conv, w_down, w_ple, g_ple, g_pg, w_pg, b_pg):
    lambda_init = 0.8 - 0.6 * math.exp(-0.3 * depth_index)
    row = lambda a: a.reshape(1, -1).astype(F32)

    g_mix = row(g_mix)
    qkv, y_conv = _inproj_call(x, g_mix, w_in.astype(F32), _qk_table(g_q, g_k),
                               w_conv_mix.astype(F32), tm=1024)
    later = [w_in, w_in, w_att_out, w_conv_out, w_o, w_up, w_down, w_ple, w_pg]
    jobs = ([(D_MODEL, OFF_GA // D_MODEL), (D_MODEL, OFF_GC // D_MODEL)]
            + [(w.shape[1], 0) for w in later[2:]])
    y_att, w_ga, w_gc, w_att_out, w_conv_out, w_o, w_up, w_down, w_ple, w_pg = _attention(
        lam_rows, row(g_sub), g_q, g_k, qkv, [w.astype(F32) for w in later], jobs,
        t=1024, lambda_init=lambda_init)
    x1, h2 = _merge_call(x, g_mix, y_att, y_conv, w_ga, w_gc, b_gates.astype(F32),
                         w_att_out, w_conv_out, w_o, row(g_ffn), tm=256)
    y_ffn = _ffn_call(h2, w_up, w_ffn_conv.astype(F32), w_down, tm=1024, tf=512)
    return _ple_call(x1, y_ffn, p, w_ple, row(g_ple), row(g_pg), w_pg, row(b_pg), tm=512)


def kernel(x, p, g_mix, w_in, b_gates, g_q, g_k, lam_q1, lam_k1, lam_q2, lam_k2, g_sub, w_conv_mix,
           w_att_out, w_conv_out, w_o, g_ffn, w_up, w_ffn_conv, w_down, w_ple, g_ple, g_pg, w_pg,
           b_pg):
    batch, seq, d_model = x.shape
    depth = p.shape[0]
    outs = []
    for b in range(batch):
        xb = x.reshape(seq, d_model) if batch == 1 else x[b]
        for i in range(depth):
            lam_rows = jnp.stack([lam_q1[i], lam_k1[i], lam_q2[i], lam_k2[i]]).astype(F32)
            pb = p.reshape(seq, PLE_DIM) if batch == 1 and depth == 1 else p[i, b]
            xb = _layer(i, xb, pb, g_mix[i], w_in[i], b_gates[i], g_q[i], g_k[i], lam_rows,
                        g_sub[i], w_conv_mix[i], w_att_out[i], w_conv_out[i], w_o[i], g_ffn[i],
                        w_up[i], w_ffn_conv[i], w_down[i], w_ple[i], g_ple[i], g_pg[i], w_pg[i],
                        b_pg[i])
        outs.append(xb)
    return outs[0].reshape(x.shape) if batch == 1 else jnp.stack(outs)
```
